```python
import jax, jax.numpy as jnp
from jax import lax
import numpy as np

D_MODEL = 4096
BATCH = 1
SEQ = 16384
DEPTH = 4

CHUNK = 64
Q_BLOCK = 128
GDN_HEAD_DIM = 128
GDN_HEADS = D_MODEL // 256
GDN_WIDTH = GDN_HEADS * GDN_HEAD_DIM
SHORT_CONV = 4
FOX_HEAD_DIM = 128
FOX_HEADS = D_MODEL // 256
FOX_WIDTH = FOX_HEADS * FOX_HEAD_DIM
D_FF = 2 * D_MODEL
FFN_CONV = 3
NORM_EPS = 1e-6
SPLIT_SIZES = (3 * GDN_WIDTH, GDN_WIDTH, GDN_HEADS, GDN_HEADS,
               3 * FOX_WIDTH, FOX_HEADS, D_MODEL, D_MODEL)
N_IN = 3 * GDN_WIDTH + GDN_WIDTH + 2 * GDN_HEADS + 3 * FOX_WIDTH + FOX_HEADS + 2 * D_MODEL

kernel_name = "gdn_fox_gated_hybrid_convffn"


def _rmsnorm(x, g):
    xf = x.astype(jnp.float32)
    r = lax.rsqrt(jnp.mean(xf * xf, axis=-1, keepdims=True) + NORM_EPS)
    return (xf * r).astype(x.dtype) * g


def _l2norm(x):
    xf = x.astype(jnp.float32)
    return xf * lax.rsqrt(jnp.sum(xf * xf, axis=-1, keepdims=True) + NORM_EPS)


def _causal_depthwise_conv(u, w):
    K = w.shape[0]
    S = u.shape[1]
    up = jnp.pad(u, ((0, 0), (K - 1, 0), (0, 0)))
    y = w[K - 1] * u
    for k in range(K - 1):
        y = y + w[k] * up[:, k:k + S]
    return y


def _gated_delta_net(q, k, v, log_alpha, beta):
    f32 = jnp.float32
    B, S, H, Dk = q.shape
    Dv = v.shape[-1]
    N = S // CHUNK
    q = _l2norm(q) * (Dk ** -0.5)
    k = _l2norm(k)
    v = v.astype(f32)

    def to_chunks(t):
        return t.reshape(B, N, CHUNK, H, -1).transpose(0, 3, 1, 2, 4)

    q, k, v = to_chunks(q), to_chunks(k), to_chunks(v)
    g = log_alpha.astype(f32).reshape(B, N, CHUNK, H).transpose(0, 3, 1, 2)
    bt = beta.astype(f32).reshape(B, N, CHUNK, H).transpose(0, 3, 1, 2)
    g = jnp.cumsum(g, axis=-1)
    causal = jnp.tril(jnp.ones((CHUNK, CHUNK), dtype=bool))
    strict = jnp.tril(jnp.ones((CHUNK, CHUNK), dtype=bool), -1)
    decay = jnp.exp(jnp.where(causal, g[..., :, None] - g[..., None, :], -jnp.inf))

    kk = jnp.einsum('bhncd,bhnsd->bhncs', k, k)
    a_low = jnp.where(strict, bt[..., :, None] * kk * decay, 0.0)
    lhs = jnp.eye(CHUNK, dtype=f32) + a_low
    rhs = jnp.concatenate([(bt * jnp.exp(g))[..., None] * k, bt[..., None] * v], axis=-1)
    sol = lax.linalg.triangular_solve(lhs, rhs, left_side=True, lower=True, unit_diagonal=True)
    w_c, u_c = sol[..., :Dk], sol[..., Dk:]

    qk = jnp.einsum('bhncd,bhnsd->bhncs', q, k) * decay
    q_dec = q * jnp.exp(g)[..., None]
    k_dec = k * jnp.exp(g[..., -1:] - g)[..., None]
    g_last = jnp.exp(g[..., -1])

    def step(state, inp):
        qd, qkc, kd, wc, uc, gl = inp
        v_new = uc - jnp.einsum('bhcd,bhde->bhce', wc, state)
        o = (jnp.einsum('bhcd,bhde->bhce', qd, state)
             + jnp.einsum('bhcs,bhse->bhce', qkc, v_new))
        state = gl[..., None, None] * state + jnp.einsum('bhcd,bhce->bhde', kd, v_new)
        return state, o

    xs = tuple(jnp.moveaxis(t, 2, 0) for t in (q_dec, qk, k_dec, w_c, u_c, g_last))
    s0 = jnp.zeros((B, H, Dk, Dv), f32)
    _, o = lax.scan(step, s0, xs)
    return o.transpose(1, 0, 3, 2, 4).reshape(B, S, H, Dv)


def _forgetting_attention(q, k, v, log_f):
    f32 = jnp.float32
    B, S, H, D = q.shape
    nb = S // Q_BLOCK
    c = jnp.cumsum(log_f.astype(f32), axis=1).transpose(0, 2, 1)
    qb = (q.astype(f32) * (D ** -0.5)).reshape(B, nb, Q_BLOCK, H, D).transpose(1, 0, 2, 3, 4)
    cb = c.reshape(B, H, nb, Q_BLOCK).transpose(2, 0, 1, 3)
    kf = k.astype(f32)
    vf = v.astype(f32)
    key_pos = jnp.arange(S)

    def block(args):
        i, q_i, c_i = args
        s = jnp.einsum('bqhd,bkhd->bhqk', q_i, kf) + (c_i[..., :, None] - c[:, :, None, :])
        q_pos = i * Q_BLOCK + jnp.arange(Q_BLOCK)
        s = jnp.where(key_pos[None, :] <= q_pos[:, None], s, -jnp.inf)
        p = jax.nn.softmax(s, axis=-1)
        return jnp.einsum('bhqk,bkhd->bqhd', p, vf)

    o = lax.map(block, (jnp.arange(nb), qb, cb))
    return o.transpose(1, 0, 2, 3, 4).reshape(B, S, H, D)


def _mixer(h, w_in, gdn_conv, gdn_a_log, gdn_dt_bias, gdn_norm, fox_f_bias,
           w_proj_a, w_proj_b, w_out):
    B, S, _ = h.shape
    f32 = jnp.float32
    z = h @ w_in
    points, acc = [], 0
    for sz in SPLIT_SIZES[:-1]:
        acc += sz
        points.append(acc)
    g_qkv, g_gate, g_a, g_b, f_qkv, f_f, gate_a, gate_b = jnp.split(z, points, axis=-1)

    qkv = jax.nn.silu(_causal_depthwise_conv(g_qkv, gdn_conv))
    gq, gk, gv = jnp.split(qkv, 3, axis=-1)
    log_alpha = -jnp.exp(gdn_a_log.astype(f32)) * jax.nn.softplus(g_a.astype(f32) + gdn_dt_bias.astype(f32))
    beta = jax.nn.sigmoid(g_b.astype(f32))
    o_a = _gated_delta_net(gq.reshape(B, S, GDN_HEADS, GDN_HEAD_DIM),
                           gk.reshape(B, S, GDN_HEADS, GDN_HEAD_DIM),
                           gv.reshape(B, S, GDN_HEADS, GDN_HEAD_DIM), log_alpha, beta)
    o_a = _rmsnorm(o_a.astype(h.dtype), gdn_norm) * jax.nn.silu(g_gate.reshape(B, S, GDN_HEADS, GDN_HEAD_DIM))
    o_a = o_a.reshape(B, S, GDN_WIDTH)

    fq, fk, fv = jnp.split(f_qkv, 3, axis=-1)
    log_f = jax.nn.log_sigmoid(f_f.astype(f32) + fox_f_bias.astype(f32))
    o_b = _forgetting_attention(fq.reshape(B, S, FOX_HEADS, FOX_HEAD_DIM),
                                fk.reshape(B, S, FOX_HEADS, FOX_HEAD_DIM),
                                fv.reshape(B, S, FOX_HEADS, FOX_HEAD_DIM), log_f)
    o_b = o_b.astype(h.dtype).reshape(B, S, FOX_WIDTH)

    y = jax.nn.sigmoid(gate_a) * (o_a @ w_proj_a) + jax.nn.sigmoid(gate_b) * (o_b @ w_proj_b)
    return y @ w_out


def _conv_ffn(h, w_up, ffn_conv, w_down):
    u = _causal_depthwise_conv(h @ w_up, ffn_conv)
    a, b = jnp.split(u, 2, axis=-1)
    return (jax.nn.silu(a) * b) @ w_down


def setup_inputs(seed: int = 0) -> dict:
    key = jax.random.key(seed)
    ks = jax.random.split(key, 17)
    f32 = jnp.float32
    nrm = lambda k, shape, s: jax.random.normal(k, shape, f32) * s
    dt = jnp.exp(jax.random.uniform(ks[4], (DEPTH, GDN_HEADS), f32, np.log(1e-3), np.log(1e-1)))
    return {
        "x": nrm(ks[0], (BATCH, SEQ, D_MODEL), 1.0),
        "mix_norm": 1.0 + nrm(ks[1], (DEPTH, D_MODEL), 0.02),
        "w_in": nrm(ks[2], (DEPTH, D_MODEL, N_IN), D_MODEL ** -0.5),
        "gdn_conv": nrm(ks[3], (DEPTH, SHORT_CONV, 3 * GDN_WIDTH), SHORT_CONV ** -0.5),
        "gdn_a_log": jnp.log(jax.random.uniform(ks[5], (DEPTH, GDN_HEADS), f32, 1.0, 16.0)),
        "gdn_dt_bias": jnp.log(jnp.expm1(dt)),
        "gdn_norm": 1.0 + nrm(ks[6], (DEPTH, GDN_HEAD_DIM), 0.02),
        "fox_f_bias": 2.0 + nrm(ks[7], (DEPTH, FOX_HEADS), 1.0),
        "w_proj_a": nrm(ks[8], (DEPTH, GDN_WIDTH, D_MODEL), GDN_WIDTH ** -0.5),
        "w_proj_b": nrm(ks[9], (DEPTH, FOX_WIDTH, D_MODEL), FOX_WIDTH ** -0.5),
        "w_out": nrm(ks[10], (DEPTH, D_MODEL, D_MODEL), D_MODEL ** -0.5),
        "ffn_norm": 1.0 + nrm(ks[11], (DEPTH, D_MODEL), 0.02),
        "w_up": nrm(ks[12], (DEPTH, D_MODEL, 2 * D_FF), D_MODEL ** -0.5),
        "ffn_conv": nrm(ks[13], (DEPTH, FFN_CONV, 2 * D_FF), FFN_CONV ** -0.5),
        "w_down": nrm(ks[14], (DEPTH, D_FF, D_MODEL), D_FF ** -0.5),
        "final_norm": 1.0 + nrm(ks[15], (D_MODEL,), 0.02),
    }


def reference(x, mix_norm, w_in, gdn_conv, gdn_a_log, gdn_dt_bias, gdn_norm, fox_f_bias,
              w_proj_a, w_proj_b, w_out, ffn_norm, w_up, ffn_conv, w_down, final_norm):
    h = x
    for l in range(DEPTH):
        h = h + _mixer(_rmsnorm(h, mix_norm[l]), w_in[l], gdn_conv[l], gdn_a_log[l],
                       gdn_dt_bias[l], gdn_norm[l], fox_f_bias[l], w_proj_a[l],
                       w_proj_b[l], w_out[l])
        h = h + _conv_ffn(_rmsnorm(h, ffn_norm[l]), w_up[l], ffn_conv[l], w_down[l])
    return _rmsnorm(h, final_norm)
```

```python
import functools
import math

import jax
import jax.numpy as jnp
from jax import lax
from jax.experimental import pallas as pl
from jax.experimental.pallas import tpu as pltpu

F32 = jnp.float32
BF16 = jnp.bfloat16

HEAD_DIM = 128
CHUNK = 64
SHORT_CONV = 4
FFN_CONV = 3
NORM_EPS = 1e-6
LANES = 128
HALO = 16
VMEM_LIMIT = 56 * 1024 * 1024


def _cparams(*sem):
    return pltpu.CompilerParams(dimension_semantics=sem, vmem_limit_bytes=VMEM_LIMIT)


def _tile(n, prefs):
    for t in prefs:
        if n % t == 0:
            return t
    return n


def _dot(a, b):
    return jnp.dot(a, b, preferred_element_type=F32)


def _dot_nt(a, b):
    return lax.dot_general(a, b, (((1,), (1,)), ((), ())), preferred_element_type=F32)


def _dot_tn(a, b):
    return lax.dot_general(a, b, (((0,), (0,)), ((), ())), preferred_element_type=F32)


def _split3(x):
    x1 = x.astype(BF16)
    r1 = x - x1.astype(F32)
    x2 = r1.astype(BF16)
    r2 = r1 - x2.astype(F32)
    return x1, x2, r2.astype(BF16)


def _dot_exact_rhs(sel, x):
    x1, x2, x3 = _split3(x)
    return _dot(sel, x1) + _dot(sel, x2) + _dot(sel, x3)


def _dot_exact_lhs(x, sel):
    x1, x2, x3 = _split3(x)
    return _dot(x1, sel) + _dot(x2, sel) + _dot(x3, sel)


def _sigmoid(x):
    return 1.0 / (1.0 + jnp.exp(-x))


def _silu(x):
    return x * _sigmoid(x)


def _softplus(x):
    return jnp.maximum(x, 0.0) + jnp.log1p(jnp.exp(-jnp.abs(x)))


def _rmsnorm_kernel(x_ref, g_ref, o_ref):
    x = x_ref[...]
    r = lax.rsqrt(jnp.mean(x * x, axis=-1, keepdims=True) + NORM_EPS)
    o_ref[...] = ((x * r) * g_ref[...]).astype(o_ref.dtype)


def _rmsnorm(x, g, out_dtype):
    s, d = x.shape
    tm = _tile(s, (256, 128, 64, 32, 16))
    return pl.pallas_call(
        _rmsnorm_kernel,
        out_shape=jax.ShapeDtypeStruct((s, d), out_dtype),
        grid=(s // tm,),
        in_specs=[pl.BlockSpec((tm, d), lambda i: (i, 0)),
                  pl.BlockSpec((1, d), lambda i: (0, 0))],
        out_specs=pl.BlockSpec((tm, d), lambda i: (i, 0)),
        compiler_params=_cparams("parallel"),
        name="rmsnorm",
    )(x, g.reshape(1, d))


def _mm_kernel(x_ref, w_ref, o_ref):
    o_ref[...] = _dot(x_ref[...], w_ref[...]).astype(o_ref.dtype)


def _matmul(x, w, out_dtype, name):
    m, k = x.shape
    n = w.shape[1]
    tm = _tile(m, (1024, 512, 256, 128))
    tn = _tile(n, (512, 256, 128))
    return pl.pallas_call(
        _mm_kernel,
        out_shape=jax.ShapeDtypeStruct((m, n), out_dtype),
        grid=(m // tm, n // tn),
        in_specs=[pl.BlockSpec((tm, k), lambda i, j: (i, 0)),
                  pl.BlockSpec((k, tn), lambda i, j: (0, j))],
        out_specs=pl.BlockSpec((tm, tn), lambda i, j: (i, j)),
        compiler_params=_cparams("parallel", "arbitrary"),
        name=name,
    )(x, w)


def _mm_res_kernel(x_ref, w_ref, r_ref, o_ref):
    o_ref[...] = r_ref[...] + _dot(x_ref[...], w_ref[...])


def _matmul_residual(x, w, res, name):
    m, k = x.shape
    n = w.shape[1]
    tm = _tile(m, (512, 256, 128))
    tn = _tile(n, (512, 256, 128))
    return pl.pallas_call(
        _mm_res_kernel,
        out_shape=jax.ShapeDtypeStruct((m, n), F32),
        grid=(m // tm, n // tn),
        in_specs=[pl.BlockSpec((tm, k), lambda i, j: (i, 0)),
                  pl.BlockSpec((k, tn), lambda i, j: (0, j)),
                  pl.BlockSpec((tm, tn), lambda i, j: (i, j))],
        out_specs=pl.BlockSpec((tm, tn), lambda i, j: (i, j)),
        compiler_params=_cparams("parallel", "arbitrary"),
        name=name,
    )(x, w, res)


def _gates_kernel(hn_ref, ws_ref, prm_ref, sel_ref, gexp_ref, bexp_ref, cexp_ref, rows_ref,
                  carry_ref, *, heads, tm):
    i = pl.program_id(0)

    @pl.when(i == 0)
    def _():
        carry_ref[...] = jnp.zeros_like(carry_ref)

    zs = _dot(hn_ref[...], ws_ref[...])
    lane = lax.broadcasted_iota(jnp.int32, zs.shape, 1)
    in_a = lane < heads
    in_b = (lane >= heads) & (lane < 2 * heads)
    in_f = (lane >= 2 * heads) & (lane < 3 * heads)
    a_log = prm_ref[0:1, :]
    dt_b = prm_ref[1:2, :]
    f_b = prm_ref[2:3, :]
    log_alpha = jnp.where(in_a, -jnp.exp(a_log) * _softplus(zs + dt_b), 0.0)
    beta = jnp.where(in_b, _sigmoid(zs), 0.0)
    log_f = jnp.where(in_f, -_softplus(-(zs + f_b)), 0.0)

    row = lax.broadcasted_iota(jnp.int32, (tm, tm), 0)
    col = lax.broadcasted_iota(jnp.int32, (tm, tm), 1)
    tri = col <= row
    tri_all = jnp.where(tri, 1.0, 0.0).astype(BF16)
    tri_chunk = jnp.where(tri & ((row // CHUNK) == (col // CHUNK)), 1.0, 0.0).astype(BF16)

    g_cum = _dot_exact_rhs(tri_chunk, log_alpha)
    c_cum = _dot_exact_rhs(tri_all, log_f) + jnp.where(in_f, carry_ref[...], 0.0)
    carry_ref[...] = c_cum[tm - 1:tm, :]

    packed = g_cum + beta + c_cum
    gw = gexp_ref.shape[1]
    gexp_ref[...] = _dot_exact_lhs(packed, sel_ref[:, 0:gw])
    bexp_ref[...] = _dot_exact_lhs(packed, sel_ref[:, gw:2 * gw])
    cexp_ref[...] = _dot_exact_lhs(packed, sel_ref[:, 2 * gw:3 * gw])
    rows_ref[...] = packed.T


def _gates(hn, w_small, prm, sel, heads):
    s, d = hn.shape
    gw = heads * HEAD_DIM
    tm = _tile(s, (256, 128, 64))
    exp_spec = pl.BlockSpec((tm, gw), lambda i: (i, 0))
    return pl.pallas_call(
        functools.partial(_gates_kernel, heads=heads, tm=tm),
        out_shape=(jax.ShapeDtypeStruct((s, gw), F32),) * 3 + (jax.ShapeDtypeStruct((LANES, s), F32),),
        grid=(s // tm,),
        in_specs=[pl.BlockSpec((tm, d), lambda i: (i, 0)),
                  pl.BlockSpec((d, LANES), lambda i: (0, 0)),
                  pl.BlockSpec((8, LANES), lambda i: (0, 0)),
                  pl.BlockSpec((LANES, 3 * gw), lambda i: (0, 0))],
        out_specs=(exp_spec, exp_spec, exp_spec, pl.BlockSpec((LANES, tm), lambda i: (0, i))),
        scratch_shapes=[pltpu.VMEM((1, LANES), F32)],
        compiler_params=_cparams("arbitrary"),
        name="gates",
    )(hn, w_small, prm, sel)


def _gdn_prep_kernel(q_ref, k_ref, v_ref, qh_ref, kh_ref, vh_ref, cq_ref, ck_ref, cv_ref,
                     gexp_ref, bexp_ref, rows_ref,
                     kw_ref, b_ref, qeff_ref, o0_ref, gl_ref, *, tm):
    i = pl.program_id(0)
    h = pl.program_id(1)
    first = i == 0

    def conv_silu(x_ref, halo_ref, w_ref):
        x = x_ref[...].astype(F32)
        halo = jnp.where(first, 0.0, halo_ref[...].astype(F32))
        xe = jnp.concatenate([halo, x], axis=0)
        w = w_ref[...]
        y = w[SHORT_CONV - 1:SHORT_CONV, :] * x
        for kk in range(SHORT_CONV - 1):
            off = HALO - (SHORT_CONV - 1) + kk
            y = y + w[kk:kk + 1, :] * xe[off:off + tm, :]
        return _silu(y)

    q = conv_silu(q_ref, qh_ref, cq_ref)
    k = conv_silu(k_ref, kh_ref, ck_ref)
    v = conv_silu(v_ref, vh_ref, cv_ref)
    q = q * (lax.rsqrt(jnp.sum(q * q, axis=-1, keepdims=True) + NORM_EPS) * (HEAD_DIM ** -0.5))
    k = k * lax.rsqrt(jnp.sum(k * k, axis=-1, keepdims=True) + NORM_EPS)

    g_all = gexp_ref[...]
    bt_all = bexp_ref[...]
    g_rows = rows_ref[pl.ds(h % 8, 1), :]

    ii = lax.broadcasted_iota(jnp.int32, (CHUNK, CHUNK), 0)
    jj = lax.broadcasted_iota(jnp.int32, (CHUNK, CHUNK), 1)
    causal = jj <= ii
    strict = jj < ii
    eye = jnp.where(ii == jj, 1.0, 0.0)

    for c in range(tm // CHUNK):
        sl = slice(c * CHUNK, (c + 1) * CHUNK)
        qc, kc, vc = q[sl], k[sl], v[sl]
        g = g_all[sl]
        bt = bt_all[sl]
        g_row = g_rows[:, sl]
        g_last = g[CHUNK - 1:CHUNK, :]
        decay = jnp.exp(jnp.where(causal, g[:, 0:CHUNK] - g_row, -jnp.inf))
        kb = kc.astype(BF16)
        qkk = _dot_nt(jnp.concatenate([qc, kc], axis=0).astype(BF16), kb)
        qk = qkk[0:CHUNK] * decay
        a_low = jnp.where(strict, bt[:, 0:CHUNK] * qkk[CHUNK:2 * CHUNK] * decay, 0.0)
        p = -a_low
        t = eye + p
        steps = CHUNK.bit_length() - 2
        for _ in range(steps):
            pb = p.astype(BF16)
            p = _dot(pb, pb)
            t = t + _dot(t.astype(BF16), p.astype(BF16))
        eg = jnp.exp(g)
        rhs = jnp.concatenate([(bt * eg) * kc, bt * vc], axis=1).astype(BF16)
        wu = _dot(t.astype(BF16), rhs)
        wub = wu.astype(BF16)
        k_dec = (kc * jnp.exp(g_last - g)).astype(BF16)
        kwb = _dot_tn(k_dec, wub)
        qo = _dot(qk.astype(BF16), wub)
        kw_ref[c] = kwb[:, 0:HEAD_DIM].astype(BF16)
        b_ref[c] = kwb[:, HEAD_DIM:2 * HEAD_DIM]
        qeff_ref[sl, :] = (qc * eg - qo[:, 0:HEAD_DIM]).astype(BF16)
        o0_ref[sl, :] = qo[:, HEAD_DIM:2 * HEAD_DIM]
        gl_ref[c] = jnp.exp(g_last)


def _gdn_prep(z, conv_w, gexp, bexp, rows, heads):
    s = z.shape[0]
    gw = heads * HEAD_DIM
    nc = s // CHUNK
    tm = _tile(s, (256, 128, 64))
    cpt = tm // CHUNK
    hb = tm // HALO

    def col(off):
        return lambda i, h: (i, off + h)

    def halo(off):
        return lambda i, h: (jnp.maximum(i * hb - 1, 0), off + h)

    def wcol(off):
        return lambda i, h: (0, off + h)

    blk = pl.BlockSpec((tm, HEAD_DIM), col(0))
    return pl.pallas_call(
        functools.partial(_gdn_prep_kernel, tm=tm),
        out_shape=(jax.ShapeDtypeStruct((nc, HEAD_DIM, gw), BF16),
                   jax.ShapeDtypeStruct((nc, HEAD_DIM, gw), F32),
                   jax.ShapeDtypeStruct((s, gw), BF16),
                   jax.ShapeDtypeStruct((s, gw), F32),
                   jax.ShapeDtypeStruct((nc, 1, gw), F32)),
        grid=(s // tm, heads),
        in_specs=[pl.BlockSpec((tm, HEAD_DIM), col(0)),
                  pl.BlockSpec((tm, HEAD_DIM), col(heads)),
                  pl.BlockSpec((tm, HEAD_DIM), col(2 * heads)),
                  pl.BlockSpec((HALO, HEAD_DIM), halo(0)),
                  pl.BlockSpec((HALO, HEAD_DIM), halo(heads)),
                  pl.BlockSpec((HALO, HEAD_DIM), halo(2 * heads)),
                  pl.BlockSpec((SHORT_CONV, HEAD_DIM), wcol(0)),
                  pl.BlockSpec((SHORT_CONV, HEAD_DIM), wcol(heads)),
                  pl.BlockSpec((SHORT_CONV, HEAD_DIM), wcol(2 * heads)),
                  blk, blk,
                  pl.BlockSpec((8, tm), lambda i, h: (h // 8, i))],
        out_specs=(pl.BlockSpec((cpt, HEAD_DIM, HEAD_DIM), lambda i, h: (i, 0, h)),
                   pl.BlockSpec((cpt, HEAD_DIM, HEAD_DIM), lambda i, h: (i, 0, h)),
                   blk, blk,
                   pl.BlockSpec((cpt, 1, HEAD_DIM), lambda i, h: (i, 0, h))),
        compiler_params=_cparams("parallel", "parallel"),
        name="gdn_prep",
    )(z, z, z, z, z, z, conv_w, conv_w, conv_w, gexp, bexp, rows)


def _gdn_scan_kernel(kw_ref, b_ref, qeff_ref, o0_ref, gl_ref, gate_ref, gn_ref, o_ref, s_ref,
                     *, cpt, group):
    @pl.when(pl.program_id(1) == 0)
    def _():
        s_ref[...] = jnp.zeros_like(s_ref)

    gn = gn_ref[...]
    for c in range(cpt):
        rows = slice(c * CHUNK, (c + 1) * CHUNK)
        for g in range(group):
            cols = slice(g * HEAD_DIM, (g + 1) * HEAD_DIM)
            st = s_ref[g]
            sb = st.astype(BF16)
            o = _dot(qeff_ref[rows, cols], sb) + o0_ref[rows, cols]
            s_ref[g] = gl_ref[c, :, cols] * st + (b_ref[c, :, cols] - _dot(kw_ref[c, :, cols], sb))
            r = lax.rsqrt(jnp.mean(o * o, axis=-1, keepdims=True) + NORM_EPS)
            o_ref[rows, cols] = ((o * r) * gn * _silu(gate_ref[rows, cols].astype(F32))).astype(o_ref.dtype)


def _gdn_scan(kw, b, qeff, o0, gl, z, gn, heads):
    s = qeff.shape[0]
    gw = heads * HEAD_DIM
    tm = _tile(s, (256, 128, 64))
    cpt = tm // CHUNK
    group = 8 if heads % 8 == 0 else heads
    gwid = group * HEAD_DIM
    gate_off = 3 * gw // gwid
    mat = pl.BlockSpec((cpt, HEAD_DIM, gwid), lambda hg, i: (i, 0, hg))
    blk = pl.BlockSpec((tm, gwid), lambda hg, i: (i, hg))
    return pl.pallas_call(
        functools.partial(_gdn_scan_kernel, cpt=cpt, group=group),
        out_shape=jax.ShapeDtypeStruct((s, gw), BF16),
        grid=(heads // group, s // tm),
        in_specs=[mat, mat, blk, blk,
                  pl.BlockSpec((cpt, 1, gwid), lambda hg, i: (i, 0, hg)),
                  pl.BlockSpec((tm, gwid), lambda hg, i: (i, gate_off + hg)),
                  pl.BlockSpec((1, HEAD_DIM), lambda hg, i: (0, 0))],
        out_specs=blk,
        scratch_shapes=[pltpu.VMEM((group, HEAD_DIM, HEAD_DIM), F32)],
        compiler_params=_cparams("parallel", "arbitrary"),
        name="gdn_scan",
    )(kw, b, qeff, o0, gl, z, gn.reshape(1, HEAD_DIM))


def _fox_kernel(q_ref, k_ref, v_ref, cq_ref, rows_ref, o_ref, *, heads, t):
    h = pl.program_id(0)
    i = pl.program_id(1)
    q = (q_ref[...].astype(F32) * (HEAD_DIM ** -0.5)).astype(BF16)
    cq = cq_ref[...][:, 0:1]
    r = (2 * heads + h) % 8

    def block(j, carry, masked):
        m, l, acc = carry
        start = pl.multiple_of(j * t, t)
        kj = k_ref[pl.ds(start, t), :]
        vj = v_ref[pl.ds(start, t), :]
        ck = rows_ref[pl.ds(r, 1), pl.ds(start, t)]
        s = _dot_nt(q, kj) + (cq - ck)
        if masked:
            qi = lax.broadcasted_iota(jnp.int32, (t, t), 0)
            ki = lax.broadcasted_iota(jnp.int32, (t, t), 1)
            s = jnp.where(ki <= qi, s, -jnp.inf)
        m_new = jnp.maximum(m, jnp.max(s, axis=-1, keepdims=True))
        p = jnp.exp(s - m_new)
        alpha = jnp.exp(m - m_new)
        l = alpha * l + jnp.sum(p, axis=-1, keepdims=True)
        acc = alpha * acc + _dot(p.astype(BF16), vj)
        return m_new, l, acc

    init = (jnp.full((t, 1), -jnp.inf, F32), jnp.zeros((t, 1), F32), jnp.zeros((t, HEAD_DIM), F32))
    carry = lax.fori_loop(0, i, lambda j, c: block(j, c, False), init)
    m, l, acc = block(i, carry, True)
    o_ref[...] = (acc / l).astype(o_ref.dtype)


def _fox(z, cexp, rows, heads):
    s = z.shape[0]
    gw = heads * HEAD_DIM
    t = _tile(s, (256, 128))
    q_off, k_off, v_off = 4 * heads, 5 * heads, 6 * heads
    return pl.pallas_call(
        functools.partial(_fox_kernel, heads=heads, t=t),
        out_shape=jax.ShapeDtypeStruct((s, gw), BF16),
        grid=(heads, s // t),
        in_specs=[pl.BlockSpec((t, HEAD_DIM), lambda h, i: (i, q_off + h)),
                  pl.BlockSpec((s, HEAD_DIM), lambda h, i: (0, k_off + h)),
                  pl.BlockSpec((s, HEAD_DIM), lambda h, i: (0, v_off + h)),
                  pl.BlockSpec((t, HEAD_DIM), lambda h, i: (i, h)),
                  pl.BlockSpec((8, s), lambda h, i: ((2 * heads + h) // 8, 0))],
        out_specs=pl.BlockSpec((t, HEAD_DIM), lambda h, i: (i, h)),
        compiler_params=_cparams("parallel", "arbitrary"),
        name="fox_attention",
    )(z, z, z, cexp, rows)


def _merge_kernel(oa_ref, ob_ref, wa_ref, wb_ref, ga_ref, gb_ref, o_ref):
    ya = _dot(oa_ref[...], wa_ref[...])
    yb = _dot(ob_ref[...], wb_ref[...])
    y = _sigmoid(ga_ref[...].astype(F32)) * ya + _sigmoid(gb_ref[...].astype(F32)) * yb
    o_ref[...] = y.astype(o_ref.dtype)


def _merge(oa, ob, wa, wb, z, heads):
    s, gw = oa.shape
    d = wa.shape[1]
    tm = _tile(s, (1024, 512, 256, 128))
    tn = _tile(math.gcd(7 * gw, d), (512, 256, 128))
    ga_off = 7 * gw // tn
    gb_off = (7 * gw + d) // tn
    return pl.pallas_call(
        _merge_kernel,
        out_shape=jax.ShapeDtypeStruct((s, d), BF16),
        grid=(s // tm, d // tn),
        in_specs=[pl.BlockSpec((tm, gw), lambda i, j: (i, 0)),
                  pl.BlockSpec((tm, gw), lambda i, j: (i, 0)),
                  pl.BlockSpec((gw, tn), lambda i, j: (0, j)),
                  pl.BlockSpec((gw, tn), lambda i, j: (0, j)),
                  pl.BlockSpec((tm, tn), lambda i, j: (i, ga_off + j)),
                  pl.BlockSpec((tm, tn), lambda i, j: (i, gb_off + j))],
        out_specs=pl.BlockSpec((tm, tn), lambda i, j: (i, j)),
        compiler_params=_cparams("parallel", "arbitrary"),
        name="merge",
    )(oa, ob, wa, wb, z, z)


def _ffn_up_kernel(x_ref, xh_ref, wa_ref, wb_ref, ca_ref, cb_ref, o_ref, xs_ref, *, tm):
    i = pl.program_id(0)

    @pl.when(pl.program_id(1) == 0)
    def _():
        xs_ref[0:HALO, :] = jnp.where(i == 0, jnp.zeros_like(xh_ref[...]), xh_ref[...])
        xs_ref[HALO:HALO + tm, :] = x_ref[...]

    xs = xs_ref[...]

    def conv(u, c_ref):
        c = c_ref[...]
        y = c[FFN_CONV - 1:FFN_CONV, :] * u[HALO:HALO + tm, :]
        for kk in range(FFN_CONV - 1):
            off = HALO - (FFN_CONV - 1) + kk
            y = y + c[kk:kk + 1, :] * u[off:off + tm, :]
        return y

    a = conv(_dot(xs, wa_ref[...]), ca_ref)
    b = conv(_dot(xs, wb_ref[...]), cb_ref)
    o_ref[...] = (_silu(a) * b).astype(o_ref.dtype)


def _ffn_up(hn, w_up, conv_w):
    s, d = hn.shape
    dff = w_up.shape[1] // 2
    tm = _tile(s, (512, 256, 128, 64))
    tn = _tile(dff, (256, 128))
    nb = dff // tn
    hb = tm // HALO
    return pl.pallas_call(
        functools.partial(_ffn_up_kernel, tm=tm),
        out_shape=jax.ShapeDtypeStruct((s, dff), BF16),
        grid=(s // tm, nb),
        in_specs=[pl.BlockSpec((tm, d), lambda i, j: (i, 0)),
                  pl.BlockSpec((HALO, d), lambda i, j: (jnp.maximum(i * hb - 1, 0), 0)),
                  pl.BlockSpec((d, tn), lambda i, j: (0, j)),
                  pl.BlockSpec((d, tn), lambda i, j: (0, nb + j)),
                  pl.BlockSpec((FFN_CONV, tn), lambda i, j: (0, j)),
                  pl.BlockSpec((FFN_CONV, tn), lambda i, j: (0, nb + j))],
        out_specs=pl.BlockSpec((tm, tn), lambda i, j: (i, j)),
        scratch_shapes=[pltpu.VMEM((HALO + tm, d), BF16)],
        compiler_params=_cparams("parallel", "arbitrary"),
        name="ffn_up",
    )(hn, hn, w_up, w_up, conv_w, conv_w)


def _mixer_weights(w_in, heads):
    d = w_in.shape[0]
    gw = heads * HEAD_DIM
    o_small = 4 * gw
    o_fqkv = o_small + 2 * heads
    o_ff = o_fqkv + 3 * gw
    o_gates = o_ff + heads
    w_main = jnp.concatenate(
        [w_in[:, 0:o_small], w_in[:, o_fqkv:o_ff], w_in[:, o_gates:o_gates + 2 * d]], axis=1).astype(BF16)
    w_small = jnp.concatenate(
        [w_in[:, o_small:o_fqkv], w_in[:, o_ff:o_gates],
         jnp.zeros((d, LANES - 3 * heads), w_in.dtype)], axis=1).astype(BF16)
    return w_main, w_small


def _gate_params(a_log, dt_bias, f_bias, heads):
    prm = jnp.zeros((8, LANES), F32)
    prm = prm.at[0, 0:heads].set(a_log.astype(F32))
    prm = prm.at[1, 0:heads].set(dt_bias.astype(F32))
    prm = prm.at[2, 2 * heads:3 * heads].set(f_bias.astype(F32))
    return prm


def _selector(heads):
    gw = heads * HEAD_DIM
    src = lax.broadcasted_iota(jnp.int32, (LANES, 3 * gw), 0)
    dst = lax.broadcasted_iota(jnp.int32, (LANES, 3 * gw), 1)
    want = (dst // gw) * heads + (dst % gw) // HEAD_DIM
    return jnp.where(src == want, 1.0, 0.0).astype(BF16)


def kernel(x, mix_norm, w_in, gdn_conv, gdn_a_log, gdn_dt_bias, gdn_norm, fox_f_bias,
           w_proj_a, w_proj_b, w_out, ffn_norm, w_up, ffn_conv, w_down, final_norm):
    batch, seq, d = x.shape
    depth = w_in.shape[0]
    heads = d // (2 * HEAD_DIM)
    assert 3 * heads <= LANES and seq % CHUNK == 0
    sel = _selector(heads)
    outs = []
    for bi in range(batch):
        h = x[bi].astype(F32)
        for l in range(depth):
            w_main, w_small = _mixer_weights(w_in[l], heads)
            hn = _rmsnorm(h, mix_norm[l], BF16)
            z = _matmul(hn, w_main, BF16, "in_proj")
            prm = _gate_params(gdn_a_log[l], gdn_dt_bias[l], fox_f_bias[l], heads)
            gexp, bexp, cexp, rows = _gates(hn, w_small, prm, sel, heads)
            kw, b, qeff, o0, gl = _gdn_prep(z, gdn_conv[l].astype(F32), gexp, bexp, rows, heads)
            oa = _gdn_scan(kw, b, qeff, o0, gl, z, gdn_norm[l].astype(F32), heads)
            ob = _fox(z, cexp, rows, heads)
            y = _merge(oa, ob, w_proj_a[l].astype(BF16), w_proj_b[l].astype(BF16), z, heads)
            h = _matmul_residual(y, w_out[l].astype(BF16), h, "out_proj")
            hn = _rmsnorm(h, ffn_norm[l], BF16)
            act = _ffn_up(hn, w_up[l].astype(BF16), ffn_conv[l].astype(F32))
            h = _matmul_residual(act, w_down[l].astype(BF16), h, "ffn_down")
        outs.append(_rmsnorm(h, final_norm, x.dtype))
    return jnp.stack(outs, axis=0)
```

```python
import functools
import math

import jax
import jax.numpy as jnp
from jax import lax
from jax.experimental import pallas as pl
from jax.experimental.pallas import tpu as pltpu

F32 = jnp.float32
BF16 = jnp.bfloat16

HEAD_DIM = 128
CHUNK = 64
SHORT_CONV = 4
FFN_CONV = 3
NORM_EPS = 1e-6
LOG2E = 1.4426950408889634
LANES = 128
HALO = 16
VMEM_LIMIT = 56 * 1024 * 1024


def _cparams(*sem):
    return pltpu.CompilerParams(dimension_semantics=sem, vmem_limit_bytes=VMEM_LIMIT)


def _tile(n, prefs):
    for t in prefs:
        if n % t == 0:
            return t
    return n


def _dot(a, b):
    return jnp.dot(a, b, preferred_element_type=F32)


def _dot_nt(a, b):
    return lax.dot_general(a, b, (((1,), (1,)), ((), ())), preferred_element_type=F32)


def _dot_tn(a, b):
    return lax.dot_general(a, b, (((0,), (0,)), ((), ())), preferred_element_type=F32)


def _split3(x):
    x1 = x.astype(BF16)
    r1 = x - x1.astype(F32)
    x2 = r1.astype(BF16)
    r2 = r1 - x2.astype(F32)
    return x1, x2, r2.astype(BF16)


def _dot_exact_rhs(sel, x):
    x1, x2, x3 = _split3(x)
    return _dot(sel, x1) + _dot(sel, x2) + _dot(sel, x3)


def _dot_exact_lhs(x, sel):
    x1, x2, x3 = _split3(x)
    return _dot(x1, sel) + _dot(x2, sel) + _dot(x3, sel)


def _sigmoid(x):
    return 1.0 / (1.0 + jnp.exp(-x))


def _silu(x):
    return x * _sigmoid(x)


def _softplus(x):
    return jnp.maximum(x, 0.0) + jnp.log1p(jnp.exp(-jnp.abs(x)))


def _rmsnorm_kernel(x_ref, g_ref, o_ref):
    x = x_ref[...]
    r = lax.rsqrt(jnp.mean(x * x, axis=-1, keepdims=True) + NORM_EPS)
    o_ref[...] = ((x * r) * g_ref[...]).astype(o_ref.dtype)


def _rmsnorm(x, g, out_dtype):
    s, d = x.shape
    tm = _tile(s, (256, 128, 64, 32, 16))
    return pl.pallas_call(
        _rmsnorm_kernel,
        out_shape=jax.ShapeDtypeStruct((s, d), out_dtype),
        grid=(s // tm,),
        in_specs=[pl.BlockSpec((tm, d), lambda i: (i, 0)),
                  pl.BlockSpec((1, d), lambda i: (0, 0))],
        out_specs=pl.BlockSpec((tm, d), lambda i: (i, 0)),
        compiler_params=_cparams("parallel"),
        name="rmsnorm",
    )(x, g.reshape(1, d))


def _mm_kernel(x_ref, w_ref, o_ref):
    o_ref[...] = _dot(x_ref[...], w_ref[...]).astype(o_ref.dtype)


def _matmul(x, w, out_dtype, name):
    m, k = x.shape
    n = w.shape[1]
    tm = _tile(m, (1024, 512, 256, 128))
    tn = _tile(n, (512, 256, 128))
    return pl.pallas_call(
        _mm_kernel,
        out_shape=jax.ShapeDtypeStruct((m, n), out_dtype),
        grid=(m // tm, n // tn),
        in_specs=[pl.BlockSpec((tm, k), lambda i, j: (i, 0)),
                  pl.BlockSpec((k, tn), lambda i, j: (0, j))],
        out_specs=pl.BlockSpec((tm, tn), lambda i, j: (i, j)),
        compiler_params=_cparams("parallel", "arbitrary"),
        name=name,
    )(x, w)


def _mm_res_kernel(x_ref, w_ref, r_ref, o_ref):
    o_ref[...] = r_ref[...] + _dot(x_ref[...], w_ref[...])


def _matmul_residual(x, w, res, name):
    m, k = x.shape
    n = w.shape[1]
    tm = _tile(m, (512, 256, 128))
    tn = _tile(n, (512, 256, 128))
    return pl.pallas_call(
        _mm_res_kernel,
        out_shape=jax.ShapeDtypeStruct((m, n), F32),
        grid=(m // tm, n // tn),
        in_specs=[pl.BlockSpec((tm, k), lambda i, j: (i, 0)),
                  pl.BlockSpec((k, tn), lambda i, j: (0, j)),
                  pl.BlockSpec((tm, tn), lambda i, j: (i, j))],
        out_specs=pl.BlockSpec((tm, tn), lambda i, j: (i, j)),
        compiler_params=_cparams("parallel", "arbitrary"),
        name=name,
    )(x, w, res)


def _gates_kernel(hn_ref, ws_ref, prm_ref, sel_ref, gexp_ref, bexp_ref, cexp_ref, rows_ref,
                  carry_ref, *, heads, tm):
    i = pl.program_id(0)

    @pl.when(i == 0)
    def _():
        carry_ref[...] = jnp.zeros_like(carry_ref)

    zs = _dot(hn_ref[...], ws_ref[...])
    lane = lax.broadcasted_iota(jnp.int32, zs.shape, 1)
    in_a = lane < heads
    in_b = (lane >= heads) & (lane < 2 * heads)
    in_f = (lane >= 2 * heads) & (lane < 3 * heads)
    a_log = prm_ref[0:1, :]
    dt_b = prm_ref[1:2, :]
    f_b = prm_ref[2:3, :]
    log_alpha = jnp.where(in_a, -jnp.exp(a_log) * _softplus(zs + dt_b), 0.0)
    beta = jnp.where(in_b, _sigmoid(zs), 0.0)
    log_f = jnp.where(in_f, -_softplus(-(zs + f_b)), 0.0)

    row = lax.broadcasted_iota(jnp.int32, (tm, tm), 0)
    col = lax.broadcasted_iota(jnp.int32, (tm, tm), 1)
    tri = col <= row
    tri_all = jnp.where(tri, 1.0, 0.0).astype(BF16)
    tri_chunk = jnp.where(tri & ((row // CHUNK) == (col // CHUNK)), 1.0, 0.0).astype(BF16)

    g_cum = _dot_exact_rhs(tri_chunk, log_alpha)
    c_cum = _dot_exact_rhs(tri_all, log_f) + jnp.where(in_f, carry_ref[...], 0.0)
    carry_ref[...] = c_cum[tm - 1:tm, :]

    packed = g_cum + beta + c_cum
    gw = gexp_ref.shape[1]
    gexp_ref[...] = _dot_exact_lhs(packed, sel_ref[:, 0:gw])
    bexp_ref[...] = _dot_exact_lhs(packed, sel_ref[:, gw:2 * gw])
    cexp_ref[...] = _dot_exact_lhs(packed, sel_ref[:, 2 * gw:3 * gw])
    rows_ref[...] = packed.T


def _gates(hn, w_small, prm, sel, heads):
    s, d = hn.shape
    gw = heads * HEAD_DIM
    tm = _tile(s, (256, 128, 64))
    exp_spec = pl.BlockSpec((tm, gw), lambda i: (i, 0))
    return pl.pallas_call(
        functools.partial(_gates_kernel, heads=heads, tm=tm),
        out_shape=(jax.ShapeDtypeStruct((s, gw), F32),) * 3 + (jax.ShapeDtypeStruct((LANES, s), F32),),
        grid=(s // tm,),
        in_specs=[pl.BlockSpec((tm, d), lambda i: (i, 0)),
                  pl.BlockSpec((d, LANES), lambda i: (0, 0)),
                  pl.BlockSpec((8, LANES), lambda i: (0, 0)),
                  pl.BlockSpec((LANES, 3 * gw), lambda i: (0, 0))],
        out_specs=(exp_spec, exp_spec, exp_spec, pl.BlockSpec((LANES, tm), lambda i: (0, i))),
        scratch_shapes=[pltpu.VMEM((1, LANES), F32)],
        compiler_params=_cparams("arbitrary"),
        name="gates",
    )(hn, w_small, prm, sel)


def _gdn_prep_kernel(q_ref, k_ref, v_ref, qh_ref, kh_ref, vh_ref, cq_ref, ck_ref, cv_ref,
                     gexp_ref, bexp_ref, rows_ref,
                     kw_ref, b_ref, qeff_ref, o0_ref, gl_ref, *, tm):
    i = pl.program_id(0)
    h = pl.program_id(1)
    first = i == 0

    def conv_silu(x_ref, halo_ref, w_ref):
        x = x_ref[...].astype(F32)
        halo = jnp.where(first, 0.0, halo_ref[...].astype(F32))
        xe = jnp.concatenate([halo, x], axis=0)
        w = w_ref[...]
        y = w[SHORT_CONV - 1:SHORT_CONV, :] * x
        for kk in range(SHORT_CONV - 1):
            off = HALO - (SHORT_CONV - 1) + kk
            y = y + w[kk:kk + 1, :] * xe[off:off + tm, :]
        return _silu(y)

    q = conv_silu(q_ref, qh_ref, cq_ref)
    k = conv_silu(k_ref, kh_ref, ck_ref)
    v = conv_silu(v_ref, vh_ref, cv_ref)
    q = q * (lax.rsqrt(jnp.sum(q * q, axis=-1, keepdims=True) + NORM_EPS) * (HEAD_DIM ** -0.5))
    k = k * lax.rsqrt(jnp.sum(k * k, axis=-1, keepdims=True) + NORM_EPS)

    g_all = gexp_ref[...]
    bt_all = bexp_ref[...]
    g_rows = rows_ref[pl.ds(h % 8, 1), :]

    ii = lax.broadcasted_iota(jnp.int32, (CHUNK, CHUNK), 0)
    jj = lax.broadcasted_iota(jnp.int32, (CHUNK, CHUNK), 1)
    causal = jj <= ii
    strict = jj < ii

    sls = [slice(c * CHUNK, (c + 1) * CHUNK) for c in range(tm // CHUNK)]
    gs = [g_all[sl] for sl in sls]
    bts = [bt_all[sl] for sl in sls]
    egs = [jnp.exp(g) for g in gs]
    g_lasts = [g[CHUNK - 1:CHUNK, :] for g in gs]
    decays = [jnp.exp(jnp.where(causal, g[:, 0:CHUNK] - g_rows[:, sl], -jnp.inf)) for g, sl in zip(gs, sls)]
    qkks = [_dot_nt(jnp.concatenate([q[sl], k[sl]], axis=0).astype(BF16), k[sl].astype(BF16)) for sl in sls]
    qks = [(qkk[0:CHUNK] * d).astype(BF16) for qkk, d in zip(qkks, decays)]
    ps = [jnp.where(strict, -(bt[:, 0:CHUNK] * qkk[CHUNK:2 * CHUNK] * d), 0.0).astype(BF16)
          for bt, qkk, d in zip(bts, qkks, decays)]
    xs = [jnp.concatenate([(bt * eg) * k[sl], bt * v[sl]], axis=1) for bt, eg, sl in zip(bts, egs, sls)]
    levels = CHUNK.bit_length() - 1
    for lev in range(levels):
        xs = [x + _dot(p, x.astype(BF16)) for p, x in zip(ps, xs)]
        if lev + 1 < levels:
            ps = [_dot(p, p).astype(BF16) for p in ps]
    wubs = [x.astype(BF16) for x in xs]
    k_decs = [(k[sl] * jnp.exp(gl - g)).astype(BF16) for sl, gl, g in zip(sls, g_lasts, gs)]
    kwbs = [_dot_tn(kd, wub) for kd, wub in zip(k_decs, wubs)]
    qos = [_dot(qk, wub) for qk, wub in zip(qks, wubs)]
    for c, sl in enumerate(sls):
        kw_ref[c] = kwbs[c][:, 0:HEAD_DIM].astype(BF16)
        b_ref[c] = kwbs[c][:, HEAD_DIM:2 * HEAD_DIM]
        qeff_ref[sl, :] = (q[sl] * egs[c] - qos[c][:, 0:HEAD_DIM]).astype(BF16)
        o0_ref[sl, :] = qos[c][:, HEAD_DIM:2 * HEAD_DIM]
        gl_ref[c] = jnp.exp(g_lasts[c])


def _gdn_prep(z, conv_w, gexp, bexp, rows, heads):
    s = z.shape[0]
    gw = heads * HEAD_DIM
    nc = s // CHUNK
    tm = _tile(s, (512, 256, 128, 64))
    cpt = tm // CHUNK
    hb = tm // HALO

    def col(off):
        return lambda i, h: (i, off + h)

    def halo(off):
        return lambda i, h: (jnp.maximum(i * hb - 1, 0), off + h)

    def wcol(off):
        return lambda i, h: (0, off + h)

    blk = pl.BlockSpec((tm, HEAD_DIM), col(0))
    return pl.pallas_call(
        functools.partial(_gdn_prep_kernel, tm=tm),
        out_shape=(jax.ShapeDtypeStruct((nc, HEAD_DIM, gw), BF16),
                   jax.ShapeDtypeStruct((nc, HEAD_DIM, gw), F32),
                   jax.ShapeDtypeStruct((s, gw), BF16),
                   jax.ShapeDtypeStruct((s, gw), F32),
                   jax.ShapeDtypeStruct((nc, 1, gw), F32)),
        grid=(s // tm, heads),
        in_specs=[pl.BlockSpec((tm, HEAD_DIM), col(0)),
                  pl.BlockSpec((tm, HEAD_DIM), col(heads)),
                  pl.BlockSpec((tm, HEAD_DIM), col(2 * heads)),
                  pl.BlockSpec((HALO, HEAD_DIM), halo(0)),
                  pl.BlockSpec((HALO, HEAD_DIM), halo(heads)),
                  pl.BlockSpec((HALO, HEAD_DIM), halo(2 * heads)),
                  pl.BlockSpec((SHORT_CONV, HEAD_DIM), wcol(0)),
                  pl.BlockSpec((SHORT_CONV, HEAD_DIM), wcol(heads)),
                  pl.BlockSpec((SHORT_CONV, HEAD_DIM), wcol(2 * heads)),
                  blk, blk,
                  pl.BlockSpec((8, tm), lambda i, h: (h // 8, i))],
        out_specs=(pl.BlockSpec((cpt, HEAD_DIM, HEAD_DIM), lambda i, h: (i, 0, h)),
                   pl.BlockSpec((cpt, HEAD_DIM, HEAD_DIM), lambda i, h: (i, 0, h)),
                   blk, blk,
                   pl.BlockSpec((cpt, 1, HEAD_DIM), lambda i, h: (i, 0, h))),
        compiler_params=_cparams("parallel", "parallel"),
        name="gdn_prep",
    )(z, z, z, z, z, z, conv_w, conv_w, conv_w, gexp, bexp, rows)


def _gdn_scan_kernel(kw_ref, b_ref, qeff_ref, o0_ref, gl_ref, gate_ref, gn_ref, o_ref, s_ref,
                     *, cpt, group):
    @pl.when(pl.program_id(1) == 0)
    def _():
        s_ref[...] = jnp.zeros_like(s_ref)

    gn = gn_ref[...]
    for c in range(cpt):
        rows = slice(c * CHUNK, (c + 1) * CHUNK)
        for g in range(group):
            cols = slice(g * HEAD_DIM, (g + 1) * HEAD_DIM)
            st = s_ref[g]
            sb = st.astype(BF16)
            o = _dot(qeff_ref[rows, cols], sb) + o0_ref[rows, cols]
            s_ref[g] = gl_ref[c, :, cols] * st + (b_ref[c, :, cols] - _dot(kw_ref[c, :, cols], sb))
            r = lax.rsqrt(jnp.mean(o * o, axis=-1, keepdims=True) + NORM_EPS)
            o_ref[rows, cols] = ((o * r) * gn * _silu(gate_ref[rows, cols].astype(F32))).astype(o_ref.dtype)


def _gdn_scan(kw, b, qeff, o0, gl, z, gn, heads):
    s = qeff.shape[0]
    gw = heads * HEAD_DIM
    tm = _tile(s, (256, 128, 64))
    cpt = tm // CHUNK
    group = 8 if heads % 8 == 0 else heads
    gwid = group * HEAD_DIM
    gate_off = 3 * gw // gwid
    mat = pl.BlockSpec((cpt, HEAD_DIM, gwid), lambda hg, i: (i, 0, hg))
    blk = pl.BlockSpec((tm, gwid), lambda hg, i: (i, hg))
    return pl.pallas_call(
        functools.partial(_gdn_scan_kernel, cpt=cpt, group=group),
        out_shape=jax.ShapeDtypeStruct((s, gw), BF16),
        grid=(heads // group, s // tm),
        in_specs=[mat, mat, blk, blk,
                  pl.BlockSpec((cpt, 1, gwid), lambda hg, i: (i, 0, hg)),
                  pl.BlockSpec((tm, gwid), lambda hg, i: (i, gate_off + hg)),
                  pl.BlockSpec((1, HEAD_DIM), lambda hg, i: (0, 0))],
        out_specs=blk,
        scratch_shapes=[pltpu.VMEM((group, HEAD_DIM, HEAD_DIM), F32)],
        compiler_params=_cparams("parallel", "arbitrary"),
        name="gdn_scan",
    )(kw, b, qeff, o0, gl, z, gn.reshape(1, HEAD_DIM))


def _fox_prep_kernel(q_ref, k_ref, v_ref, c_ref, qa_ref, ka_ref, vt_ref):
    lane = lax.broadcasted_iota(jnp.int32, c_ref.shape, 1)
    qs = (q_ref[...].astype(F32) * (HEAD_DIM ** -0.5 * LOG2E)).astype(BF16)
    qa_ref[...] = jnp.concatenate([qs, jnp.where(lane < 3, 1.0, 0.0).astype(BF16)], axis=1)
    c1, c2, c3 = _split3(c_ref[...] * (-LOG2E))
    zero = jnp.zeros_like(c1)
    pieces = jnp.where(lane == 0, c1, jnp.where(lane == 1, c2, jnp.where(lane == 2, c3, zero)))
    ka_ref[...] = jnp.concatenate([k_ref[...], pieces], axis=1)
    vt_ref[...] = v_ref[...].astype(F32).T.astype(BF16)


def _fox_prep(z, cexp, heads):
    s = z.shape[0]
    gw = heads * HEAD_DIM
    ts = _tile(s, (512, 256, 128))
    q_off, k_off, v_off = 4 * heads, 5 * heads, 6 * heads
    aug = pl.BlockSpec((ts, 2 * HEAD_DIM), lambda i, h: (i, h))
    return pl.pallas_call(
        _fox_prep_kernel,
        out_shape=(jax.ShapeDtypeStruct((s, 2 * gw), BF16),
                   jax.ShapeDtypeStruct((s, 2 * gw), BF16),
                   jax.ShapeDtypeStruct((gw, s), BF16)),
        grid=(s // ts, heads),
        in_specs=[pl.BlockSpec((ts, HEAD_DIM), lambda i, h: (i, q_off + h)),
                  pl.BlockSpec((ts, HEAD_DIM), lambda i, h: (i, k_off + h)),
                  pl.BlockSpec((ts, HEAD_DIM), lambda i, h: (i, v_off + h)),
                  pl.BlockSpec((ts, HEAD_DIM), lambda i, h: (i, h))],
        out_specs=(aug, aug, pl.BlockSpec((HEAD_DIM, ts), lambda i, h: (h, i))),
        compiler_params=_cparams("parallel", "parallel"),
        name="fox_prep",
    )(z, z, z, cexp)


def _fox_kernel(qa_ref, ka_ref, vt_ref, o_ref, sa_ref, sb_ref, m_ref, l_ref, acc_ref, *, t):
    i = pl.program_id(1)
    half = t // 2
    qa = qa_ref[...]

    def scores(j):
        start = pl.multiple_of(j * t, t)
        return _dot_nt(ka_ref[pl.ds(start, t), :], qa)

    def update(s, vt):
        m_old = m_ref[...]
        m_new = jnp.maximum(m_old, jnp.max(s, axis=0, keepdims=True))
        p = jnp.exp2(s - m_new)
        alpha = jnp.exp2(m_old - m_new)
        l_ref[...] = alpha * l_ref[...] + jnp.sum(p, axis=0, keepdims=True)
        acc_ref[...] = alpha * acc_ref[...] + _dot(vt, p.astype(BF16))
        m_ref[...] = m_new

    def block(s_ref, j, diagonal):
        for hh in range(2):
            off = pl.multiple_of(j * t + hh * half, half)
            s = s_ref[hh * half:(hh + 1) * half, :]
            if diagonal:
                key = lax.broadcasted_iota(jnp.int32, (half, t), 0) + hh * half
                qry = lax.broadcasted_iota(jnp.int32, (half, t), 1)
                s = jnp.where(key <= qry, s, -jnp.inf)
            update(s, vt_ref[:, pl.ds(off, half)])

    m_ref[...] = jnp.full(m_ref.shape, -1e30, F32)
    l_ref[...] = jnp.zeros_like(l_ref)
    acc_ref[...] = jnp.zeros_like(acc_ref)
    sa_ref[...] = scores(0)

    def body(jj, carry):
        j = 2 * jj
        sb_ref[...] = scores(j + 1)
        block(sa_ref, j, False)
        sa_ref[...] = scores(j + 2)
        block(sb_ref, j + 1, False)
        return carry

    lax.fori_loop(0, i // 2, body, 0)

    @pl.when(i % 2 == 1)
    def _():
        sb_ref[...] = scores(i)
        block(sa_ref, i - 1, False)
        block(sb_ref, i, True)

    @pl.when(i % 2 == 0)
    def _():
        block(sa_ref, i, True)

    o_ref[...] = (acc_ref[...] / l_ref[...]).T.astype(o_ref.dtype)


def _fox(qa, ka, vt, heads):
    s = qa.shape[0]
    gw = heads * HEAD_DIM
    t = _tile(s, (512, 256))
    return pl.pallas_call(
        functools.partial(_fox_kernel, t=t),
        out_shape=jax.ShapeDtypeStruct((s, gw), BF16),
        grid=(heads, s // t),
        in_specs=[pl.BlockSpec((t, 2 * HEAD_DIM), lambda h, i: (i, h)),
                  pl.BlockSpec((s, 2 * HEAD_DIM), lambda h, i: (0, h)),
                  pl.BlockSpec((HEAD_DIM, s), lambda h, i: (h, 0))],
        out_specs=pl.BlockSpec((t, HEAD_DIM), lambda h, i: (i, h)),
        scratch_shapes=[pltpu.VMEM((t, t), F32),
                        pltpu.VMEM((t, t), F32),
                        pltpu.VMEM((1, t), F32),
                        pltpu.VMEM((1, t), F32),
                        pltpu.VMEM((HEAD_DIM, t), F32)],
        compiler_params=_cparams("parallel", "arbitrary"),
        name="fox_attention",
    )(qa, ka, vt)


def _merge_kernel(oa_ref, ob_ref, wa_ref, wb_ref, ga_ref, gb_ref, o_ref):
    ya = _dot(oa_ref[...], wa_ref[...])
    yb = _dot(ob_ref[...], wb_ref[...])
    y = _sigmoid(ga_ref[...].astype(F32)) * ya + _sigmoid(gb_ref[...].astype(F32)) * yb
    o_ref[...] = y.astype(o_ref.dtype)


def _merge(oa, ob, wa, wb, z, heads):
    s, gw = oa.shape
    d = wa.shape[1]
    tm = _tile(s, (1024, 512, 256, 128))
    tn = _tile(math.gcd(7 * gw, d), (512, 256, 128))
    ga_off = 7 * gw // tn
    gb_off = (7 * gw + d) // tn
    return pl.pallas_call(
        _merge_kernel,
        out_shape=jax.ShapeDtypeStruct((s, d), BF16),
        grid=(s // tm, d // tn),
        in_specs=[pl.BlockSpec((tm, gw), lambda i, j: (i, 0)),
                  pl.BlockSpec((tm, gw), lambda i, j: (i, 0)),
                  pl.BlockSpec((gw, tn), lambda i, j: (0, j)),
                  pl.BlockSpec((gw, tn), lambda i, j: (0, j)),
                  pl.BlockSpec((tm, tn), lambda i, j: (i, ga_off + j)),
                  pl.BlockSpec((tm, tn), lambda i, j: (i, gb_off + j))],
        out_specs=pl.BlockSpec((tm, tn), lambda i, j: (i, j)),
        compiler_params=_cparams("parallel", "arbitrary"),
        name="merge",
    )(oa, ob, wa, wb, z, z)


def _ffn_up_kernel(x_ref, xh_ref, wa_ref, wb_ref, ca_ref, cb_ref, o_ref, xs_ref, *, tm):
    i = pl.program_id(0)

    @pl.when(pl.program_id(1) == 0)
    def _():
        xs_ref[0:HALO, :] = jnp.where(i == 0, jnp.zeros_like(xh_ref[...]), xh_ref[...])
        xs_ref[HALO:HALO + tm, :] = x_ref[...]

    xs = xs_ref[...]

    def conv(u, c_ref):
        c = c_ref[...]
        y = c[FFN_CONV - 1:FFN_CONV, :] * u[HALO:HALO + tm, :]
        for kk in range(FFN_CONV - 1):
            off = HALO - (FFN_CONV - 1) + kk
            y = y + c[kk:kk + 1, :] * u[off:off + tm, :]
        return y

    a = conv(_dot(xs, wa_ref[...]), ca_ref)
    b = conv(_dot(xs, wb_ref[...]), cb_ref)
    o_ref[...] = (_silu(a) * b).astype(o_ref.dtype)


def _ffn_up(hn, w_up, conv_w):
    s, d = hn.shape
    dff = w_up.shape[1] // 2
    tm = _tile(s, (512, 256, 128, 64))
    tn = _tile(dff, (256, 128))
    nb = dff // tn
    hb = tm // HALO
    return pl.pallas_call(
        functools.partial(_ffn_up_kernel, tm=tm),
        out_shape=jax.ShapeDtypeStruct((s, dff), BF16),
        grid=(s // tm, nb),
        in_specs=[pl.BlockSpec((tm, d), lambda i, j: (i, 0)),
                  pl.BlockSpec((HALO, d), lambda i, j: (jnp.maximum(i * hb - 1, 0), 0)),
                  pl.BlockSpec((d, tn), lambda i, j: (0, j)),
                  pl.BlockSpec((d, tn), lambda i, j: (0, nb + j)),
                  pl.BlockSpec((FFN_CONV, tn), lambda i, j: (0, j)),
                  pl.BlockSpec((FFN_CONV, tn), lambda i, j: (0, nb + j))],
        out_specs=pl.BlockSpec((tm, tn), lambda i, j: (i, j)),
        scratch_shapes=[pltpu.VMEM((HALO + tm, d), BF16)],
        compiler_params=_cparams("parallel", "arbitrary"),
        name="ffn_up",
    )(hn, hn, w_up, w_up, conv_w, conv_w)


def _mixer_weights(w_in, heads):
    d = w_in.shape[0]
    gw = heads * HEAD_DIM
    o_small = 4 * gw
    o_fqkv = o_small + 2 * heads
    o_ff = o_fqkv + 3 * gw
    o_gates = o_ff + heads
    w_main = jnp.concatenate(
        [w_in[:, 0:o_small], w_in[:, o_fqkv:o_ff], w_in[:, o_gates:o_gates + 2 * d]], axis=1).astype(BF16)
    w_small = jnp.concatenate(
        [w_in[:, o_small:o_fqkv], w_in[:, o_ff:o_gates],
         jnp.zeros((d, LANES - 3 * heads), w_in.dtype)], axis=1).astype(BF16)
    return w_main, w_small


def _gate_params(a_log, dt_bias, f_bias, heads):
    prm = jnp.zeros((8, LANES), F32)
    prm = prm.at[0, 0:heads].set(a_log.astype(F32))
    prm = prm.at[1, 0:heads].set(dt_bias.astype(F32))
    prm = prm.at[2, 2 * heads:3 * heads].set(f_bias.astype(F32))
    return prm


def _selector(heads):
    gw = heads * HEAD_DIM
    src = lax.broadcasted_iota(jnp.int32, (LANES, 3 * gw), 0)
    dst = lax.broadcasted_iota(jnp.int32, (LANES, 3 * gw), 1)
    want = (dst // gw) * heads + (dst % gw) // HEAD_DIM
    return jnp.where(src == want, 1.0, 0.0).astype(BF16)


def kernel(x, mix_norm, w_in, gdn_conv, gdn_a_log, gdn_dt_bias, gdn_norm, fox_f_bias,
           w_proj_a, w_proj_b, w_out, ffn_norm, w_up, ffn_conv, w_down, final_norm):
    batch, seq, d = x.shape
    depth = w_in.shape[0]
    heads = d // (2 * HEAD_DIM)
    assert 3 * heads <= LANES and seq % CHUNK == 0
    sel = _selector(heads)
    outs = []
    for bi in range(batch):
        h = x[bi].astype(F32)
        for l in range(depth):
            w_main, w_small = _mixer_weights(w_in[l], heads)
            hn = _rmsnorm(h, mix_norm[l], BF16)
            z = _matmul(hn, w_main, BF16, "in_proj")
            prm = _gate_params(gdn_a_log[l], gdn_dt_bias[l], fox_f_bias[l], heads)
            gexp, bexp, cexp, rows = _gates(hn, w_small, prm, sel, heads)
            kw, b, qeff, o0, gl = _gdn_prep(z, gdn_conv[l].astype(F32), gexp, bexp, rows, heads)
            oa = _gdn_scan(kw, b, qeff, o0, gl, z, gdn_norm[l].astype(F32), heads)
            qa, ka, vt = _fox_prep(z, cexp, heads)
            ob = _fox(qa, ka, vt, heads)
            y = _merge(oa, ob, w_proj_a[l].astype(BF16), w_proj_b[l].astype(BF16), z, heads)
            h = _matmul_residual(y, w_out[l].astype(BF16), h, "out_proj")
            hn = _rmsnorm(h, ffn_norm[l], BF16)
            act = _ffn_up(hn, w_up[l].astype(BF16), ffn_conv[l].astype(F32))
            h = _matmul_residual(act, w_down[l].astype(BF16), h, "ffn_down")
        outs.append(_rmsnorm(h, final_norm, x.dtype))
    return jnp.stack(outs, axis=0)
```

```python
import functools
import math

import jax
import jax.numpy as jnp
from jax import lax
from jax.experimental import pallas as pl
from jax.experimental.pallas import tpu as pltpu

F32 = jnp.float32
BF16 = jnp.bfloat16

HEAD_DIM = 128
CHUNK = 64
SHORT_CONV = 4
FFN_CONV = 3
NORM_EPS = 1e-6
LOG2E = 1.4426950408889634
LANES = 128
VT_ROWS = HEAD_DIM + 16
TAIL = 8
HALO = 16
VMEM_LIMIT = 56 * 1024 * 1024


def _cparams(*sem):
    return pltpu.CompilerParams(dimension_semantics=sem, vmem_limit_bytes=VMEM_LIMIT)


def _tile(n, prefs):
    for t in prefs:
        if n % t == 0:
            return t
    return n


def _dot(a, b):
    return jnp.dot(a, b, preferred_element_type=F32)


def _dot_nt(a, b):
    return lax.dot_general(a, b, (((1,), (1,)), ((), ())), preferred_element_type=F32)


def _dot_tn(a, b):
    return lax.dot_general(a, b, (((0,), (0,)), ((), ())), preferred_element_type=F32)


def _split3(x):
    x1 = x.astype(BF16)
    r1 = x - x1.astype(F32)
    x2 = r1.astype(BF16)
    r2 = r1 - x2.astype(F32)
    return x1, x2, r2.astype(BF16)


def _dot_exact_rhs(sel, x):
    x1, x2, x3 = _split3(x)
    return _dot(sel, x1) + _dot(sel, x2) + _dot(sel, x3)


def _dot_exact_lhs(x, sel):
    x1, x2, x3 = _split3(x)
    return _dot(x1, sel) + _dot(x2, sel) + _dot(x3, sel)


def _sigmoid(x):
    return 1.0 / (1.0 + jnp.exp(-x))


def _silu(x):
    return x * _sigmoid(x)


def _softplus(x):
    return jnp.maximum(x, 0.0) + jnp.log1p(jnp.exp(-jnp.abs(x)))


def _rmsnorm_kernel(x_ref, g_ref, o_ref):
    x = x_ref[...]
    r = lax.rsqrt(jnp.mean(x * x, axis=-1, keepdims=True) + NORM_EPS)
    o_ref[...] = ((x * r) * g_ref[...]).astype(o_ref.dtype)


def _rmsnorm(x, g, out_dtype):
    s, d = x.shape
    tm = _tile(s, (256, 128, 64, 32, 16))
    return pl.pallas_call(
        _rmsnorm_kernel,
        out_shape=jax.ShapeDtypeStruct((s, d), out_dtype),
        grid=(s // tm,),
        in_specs=[pl.BlockSpec((tm, d), lambda i: (i, 0)),
                  pl.BlockSpec((1, d), lambda i: (0, 0))],
        out_specs=pl.BlockSpec((tm, d), lambda i: (i, 0)),
        compiler_params=_cparams("parallel"),
        name="rmsnorm",
    )(x, g.reshape(1, d))


def _mm_kernel(x_ref, w_ref, o_ref):
    o_ref[...] = _dot(x_ref[...], w_ref[...]).astype(o_ref.dtype)


def _matmul(x, w, out_dtype, name):
    m, k = x.shape
    n = w.shape[1]
    tm = _tile(m, (1024, 512, 256, 128))
    tn = _tile(n, (1024, 512, 256, 128))
    return pl.pallas_call(
        _mm_kernel,
        out_shape=jax.ShapeDtypeStruct((m, n), out_dtype),
        grid=(m // tm, n // tn),
        in_specs=[pl.BlockSpec((tm, k), lambda i, j: (i, 0)),
                  pl.BlockSpec((k, tn), lambda i, j: (0, j))],
        out_specs=pl.BlockSpec((tm, tn), lambda i, j: (i, j)),
        compiler_params=_cparams("parallel", "arbitrary"),
        name=name,
    )(x, w)


def _mm_res_kernel(x_ref, w_ref, r_ref, o_ref):
    o_ref[...] = r_ref[...] + _dot(x_ref[...], w_ref[...])


def _matmul_residual(x, w, res, name):
    m, k = x.shape
    n = w.shape[1]
    tm = _tile(m, tuple(t for t in (1024, 512, 256, 128) if t * k <= 4 * 1024 * 1024))
    tn = _tile(n, (512, 256, 128))
    return pl.pallas_call(
        _mm_res_kernel,
        out_shape=jax.ShapeDtypeStruct((m, n), F32),
        grid=(m // tm, n // tn),
        in_specs=[pl.BlockSpec((tm, k), lambda i, j: (i, 0)),
                  pl.BlockSpec((k, tn), lambda i, j: (0, j)),
                  pl.BlockSpec((tm, tn), lambda i, j: (i, j))],
        out_specs=pl.BlockSpec((tm, tn), lambda i, j: (i, j)),
        compiler_params=_cparams("parallel", "arbitrary"),
        name=name,
    )(x, w, res)


def _gates_kernel(hn_ref, ws_ref, prm_ref, sel_ref, gexp_ref, bexp_ref, cexp_ref, rows_ref,
                  carry_ref, *, heads, tm):
    i = pl.program_id(0)

    @pl.when(i == 0)
    def _():
        carry_ref[...] = jnp.zeros_like(carry_ref)

    zs = _dot(hn_ref[...], ws_ref[...])
    lane = lax.broadcasted_iota(jnp.int32, zs.shape, 1)
    in_a = lane < heads
    in_b = (lane >= heads) & (lane < 2 * heads)
    in_f = (lane >= 2 * heads) & (lane < 3 * heads)
    a_log = prm_ref[0:1, :]
    dt_b = prm_ref[1:2, :]
    f_b = prm_ref[2:3, :]
    log_alpha = jnp.where(in_a, -jnp.exp(a_log) * _softplus(zs + dt_b), 0.0)
    beta = jnp.where(in_b, _sigmoid(zs), 0.0)
    log_f = jnp.where(in_f, -_softplus(-(zs + f_b)), 0.0)

    row = lax.broadcasted_iota(jnp.int32, (tm, tm), 0)
    col = lax.broadcasted_iota(jnp.int32, (tm, tm), 1)
    tri = col <= row
    tri_all = jnp.where(tri, 1.0, 0.0).astype(BF16)
    tri_chunk = jnp.where(tri & ((row // CHUNK) == (col // CHUNK)), 1.0, 0.0).astype(BF16)

    g_cum = _dot_exact_rhs(tri_chunk, log_alpha)
    c_cum = _dot_exact_rhs(tri_all, log_f) + jnp.where(in_f, carry_ref[...], 0.0)
    carry_ref[...] = c_cum[tm - 1:tm, :]

    packed = g_cum + beta + c_cum
    gw = gexp_ref.shape[1]
    gexp_ref[...] = _dot_exact_lhs(packed, sel_ref[:, 0:gw])
    bexp_ref[...] = _dot_exact_lhs(packed, sel_ref[:, gw:2 * gw])
    cexp_ref[...] = _dot_exact_lhs(packed, sel_ref[:, 2 * gw:3 * gw])
    rows_ref[...] = packed.T


def _gates(hn, w_small, prm, sel, heads):
    s, d = hn.shape
    gw = heads * HEAD_DIM
    tm = _tile(s, (256, 128, 64))
    exp_spec = pl.BlockSpec((tm, gw), lambda i: (i, 0))
    return pl.pallas_call(
        functools.partial(_gates_kernel, heads=heads, tm=tm),
        out_shape=(jax.ShapeDtypeStruct((s, gw), F32),) * 3 + (jax.ShapeDtypeStruct((LANES, s), F32),),
        grid=(s // tm,),
        in_specs=[pl.BlockSpec((tm, d), lambda i: (i, 0)),
                  pl.BlockSpec((d, LANES), lambda i: (0, 0)),
                  pl.BlockSpec((8, LANES), lambda i: (0, 0)),
                  pl.BlockSpec((LANES, 3 * gw), lambda i: (0, 0))],
        out_specs=(exp_spec, exp_spec, exp_spec, pl.BlockSpec((LANES, tm), lambda i: (0, i))),
        scratch_shapes=[pltpu.VMEM((1, LANES), F32)],
        compiler_params=_cparams("arbitrary"),
        name="gates",
    )(hn, w_small, prm, sel)


def _gdn_prep_kernel(q_ref, k_ref, v_ref, qh_ref, kh_ref, vh_ref, cq_ref, ck_ref, cv_ref,
                     gexp_ref, bexp_ref, rows_ref,
                     kw_ref, b_ref, qeff_ref, o0_ref, gl_ref, *, tm):
    i = pl.program_id(0)
    h = pl.program_id(1)
    first = i == 0

    def conv_silu(x_ref, halo_ref, w_ref):
        x = x_ref[...].astype(F32)
        halo = jnp.where(first, 0.0, halo_ref[...].astype(F32))
        xe = jnp.concatenate([halo, x], axis=0)
        w = w_ref[...]
        y = w[SHORT_CONV - 1:SHORT_CONV, :] * x
        for kk in range(SHORT_CONV - 1):
            off = HALO - (SHORT_CONV - 1) + kk
            y = y + w[kk:kk + 1, :] * xe[off:off + tm, :]
        return _silu(y)

    q = conv_silu(q_ref, qh_ref, cq_ref)
    k = conv_silu(k_ref, kh_ref, ck_ref)
    v = conv_silu(v_ref, vh_ref, cv_ref)
    q = q * (lax.rsqrt(jnp.sum(q * q, axis=-1, keepdims=True) + NORM_EPS) * (HEAD_DIM ** -0.5))
    k = k * lax.rsqrt(jnp.sum(k * k, axis=-1, keepdims=True) + NORM_EPS)

    g_all = gexp_ref[...]
    bt_all = bexp_ref[...]
    g_rows = rows_ref[pl.ds(h % 8, 1), :]

    ii = lax.broadcasted_iota(jnp.int32, (CHUNK, CHUNK), 0)
    jj = lax.broadcasted_iota(jnp.int32, (CHUNK, CHUNK), 1)
    causal = jj <= ii
    strict = jj < ii

    sls = [slice(c * CHUNK, (c + 1) * CHUNK) for c in range(tm // CHUNK)]
    gs = [g_all[sl] for sl in sls]
    bts = [bt_all[sl] for sl in sls]
    egs = [jnp.exp(g) for g in gs]
    g_lasts = [g[CHUNK - 1:CHUNK, :] for g in gs]
    decays = [jnp.exp(jnp.where(causal, g[:, 0:CHUNK] - g_rows[:, sl], -jnp.inf)) for g, sl in zip(gs, sls)]
    qkks = [_dot_nt(jnp.concatenate([q[sl], k[sl]], axis=0).astype(BF16), k[sl].astype(BF16)) for sl in sls]
    qks = [(qkk[0:CHUNK] * d).astype(BF16) for qkk, d in zip(qkks, decays)]
    ps = [jnp.where(strict, -(bt[:, 0:CHUNK] * qkk[CHUNK:2 * CHUNK] * d), 0.0).astype(BF16)
          for bt, qkk, d in zip(bts, qkks, decays)]
    xs = [jnp.concatenate([(bt * eg) * k[sl], bt * v[sl]], axis=1) for bt, eg, sl in zip(bts, egs, sls)]
    levels = CHUNK.bit_length() - 1
    for lev in range(levels):
        xs = [x + _dot(p, x.astype(BF16)) for p, x in zip(ps, xs)]
        if lev + 1 < levels:
            ps = [_dot(p, p).astype(BF16) for p in ps]
    wubs = [x.astype(BF16) for x in xs]
    k_decs = [(k[sl] * jnp.exp(gl - g)).astype(BF16) for sl, gl, g in zip(sls, g_lasts, gs)]
    kwbs = [_dot_tn(kd, wub) for kd, wub in zip(k_decs, wubs)]
    qos = [_dot(qk, wub) for qk, wub in zip(qks, wubs)]
    for c, sl in enumerate(sls):
        kw_ref[c] = kwbs[c][:, 0:HEAD_DIM].astype(BF16)
        b_ref[c] = kwbs[c][:, HEAD_DIM:2 * HEAD_DIM]
        qeff_ref[sl, :] = (q[sl] * egs[c] - qos[c][:, 0:HEAD_DIM]).astype(BF16)
        o0_ref[sl, :] = qos[c][:, HEAD_DIM:2 * HEAD_DIM]
        gl_ref[c] = jnp.exp(g_lasts[c])


def _gdn_prep(z, conv_w, gexp, bexp, rows, heads):
    s = z.shape[0]
    gw = heads * HEAD_DIM
    nc = s // CHUNK
    tm = _tile(s, (512, 256, 128, 64))
    cpt = tm // CHUNK
    hb = tm // HALO

    def col(off):
        return lambda i, h: (i, off + h)

    def halo(off):
        return lambda i, h: (jnp.maximum(i * hb - 1, 0), off + h)

    def wcol(off):
        return lambda i, h: (0, off + h)

    blk = pl.BlockSpec((tm, HEAD_DIM), col(0))
    return pl.pallas_call(
        functools.partial(_gdn_prep_kernel, tm=tm),
        out_shape=(jax.ShapeDtypeStruct((nc, HEAD_DIM, gw), BF16),
                   jax.ShapeDtypeStruct((nc, HEAD_DIM, gw), F32),
                   jax.ShapeDtypeStruct((s, gw), BF16),
                   jax.ShapeDtypeStruct((s, gw), F32),
                   jax.ShapeDtypeStruct((nc, 1, gw), F32)),
        grid=(s // tm, heads),
        in_specs=[pl.BlockSpec((tm, HEAD_DIM), col(0)),
                  pl.BlockSpec((tm, HEAD_DIM), col(heads)),
                  pl.BlockSpec((tm, HEAD_DIM), col(2 * heads)),
                  pl.BlockSpec((HALO, HEAD_DIM), halo(0)),
                  pl.BlockSpec((HALO, HEAD_DIM), halo(heads)),
                  pl.BlockSpec((HALO, HEAD_DIM), halo(2 * heads)),
                  pl.BlockSpec((SHORT_CONV, HEAD_DIM), wcol(0)),
                  pl.BlockSpec((SHORT_CONV, HEAD_DIM), wcol(heads)),
                  pl.BlockSpec((SHORT_CONV, HEAD_DIM), wcol(2 * heads)),
                  blk, blk,
                  pl.BlockSpec((8, tm), lambda i, h: (h // 8, i))],
        out_specs=(pl.BlockSpec((cpt, HEAD_DIM, HEAD_DIM), lambda i, h: (i, 0, h)),
                   pl.BlockSpec((cpt, HEAD_DIM, HEAD_DIM), lambda i, h: (i, 0, h)),
                   blk, blk,
                   pl.BlockSpec((cpt, 1, HEAD_DIM), lambda i, h: (i, 0, h))),
        compiler_params=_cparams("parallel", "parallel"),
        name="gdn_prep",
    )(z, z, z, z, z, z, conv_w, conv_w, conv_w, gexp, bexp, rows)


def _gdn_scan_kernel(kw_ref, b_ref, qeff_ref, o0_ref, gl_ref, gate_ref, gn_ref, o_ref, s_ref,
                     *, cpt, group):
    @pl.when(pl.program_id(1) == 0)
    def _():
        s_ref[...] = jnp.zeros_like(s_ref)

    gn = gn_ref[...]
    for c in range(cpt):
        rows = slice(c * CHUNK, (c + 1) * CHUNK)
        for g in range(group):
            cols = slice(g * HEAD_DIM, (g + 1) * HEAD_DIM)
            st = s_ref[g]
            sb = st.astype(BF16)
            o = _dot(qeff_ref[rows, cols], sb) + o0_ref[rows, cols]
            s_ref[g] = gl_ref[c, :, cols] * st + (b_ref[c, :, cols] - _dot(kw_ref[c, :, cols], sb))
            r = lax.rsqrt(jnp.mean(o * o, axis=-1, keepdims=True) + NORM_EPS)
            o_ref[rows, cols] = ((o * r) * gn * _silu(gate_ref[rows, cols].astype(F32))).astype(o_ref.dtype)


def _gdn_scan(kw, b, qeff, o0, gl, z, gn, heads):
    s = qeff.shape[0]
    gw = heads * HEAD_DIM
    tm = _tile(s, (256, 128, 64))
    cpt = tm // CHUNK
    group = 8 if heads % 8 == 0 else heads
    gwid = group * HEAD_DIM
    gate_off = 3 * gw // gwid
    mat = pl.BlockSpec((cpt, HEAD_DIM, gwid), lambda hg, i: (i, 0, hg))
    blk = pl.BlockSpec((tm, gwid), lambda hg, i: (i, hg))
    return pl.pallas_call(
        functools.partial(_gdn_scan_kernel, cpt=cpt, group=group),
        out_shape=jax.ShapeDtypeStruct((s, gw), BF16),
        grid=(heads // group, s // tm),
        in_specs=[mat, mat, blk, blk,
                  pl.BlockSpec((cpt, 1, gwid), lambda hg, i: (i, 0, hg)),
                  pl.BlockSpec((tm, gwid), lambda hg, i: (i, gate_off + hg)),
                  pl.BlockSpec((1, HEAD_DIM), lambda hg, i: (0, 0))],
        out_specs=blk,
        scratch_shapes=[pltpu.VMEM((group, HEAD_DIM, HEAD_DIM), F32)],
        compiler_params=_cparams("parallel", "arbitrary"),
        name="gdn_scan",
    )(kw, b, qeff, o0, gl, z, gn.reshape(1, HEAD_DIM))


def _fox_prep_kernel(q_ref, k_ref, v_ref, c_ref, qa_ref, ka_ref, vt_ref):
    lane = lax.broadcasted_iota(jnp.int32, c_ref.shape, 1)
    qs = (q_ref[...].astype(F32) * (HEAD_DIM ** -0.5 * LOG2E)).astype(BF16)
    qa_ref[...] = jnp.concatenate([qs, jnp.where(lane < 3, 1.0, 0.0).astype(BF16)], axis=1)
    c1, c2, c3 = _split3(c_ref[...] * (-LOG2E))
    zero = jnp.zeros_like(c1)
    pieces = jnp.where(lane == 0, c1, jnp.where(lane == 1, c2, jnp.where(lane == 2, c3, zero)))
    ka_ref[...] = jnp.concatenate([k_ref[...], pieces], axis=1)
    vt_ref[0:HEAD_DIM, :] = v_ref[...].astype(F32).T.astype(BF16)
    vt_ref[HEAD_DIM:VT_ROWS, :] = jnp.ones((VT_ROWS - HEAD_DIM, vt_ref.shape[1]), BF16)


def _fox_prep(z, cexp, heads):
    s = z.shape[0]
    gw = heads * HEAD_DIM
    ts = _tile(s, (512, 256, 128))
    q_off, k_off, v_off = 4 * heads, 5 * heads, 6 * heads
    aug = pl.BlockSpec((ts, 2 * HEAD_DIM), lambda i, h: (i, h))
    return pl.pallas_call(
        _fox_prep_kernel,
        out_shape=(jax.ShapeDtypeStruct((s, 2 * gw), BF16),
                   jax.ShapeDtypeStruct((s, 2 * gw), BF16),
                   jax.ShapeDtypeStruct((heads * VT_ROWS, s), BF16)),
        grid=(s // ts, heads),
        in_specs=[pl.BlockSpec((ts, HEAD_DIM), lambda i, h: (i, q_off + h)),
                  pl.BlockSpec((ts, HEAD_DIM), lambda i, h: (i, k_off + h)),
                  pl.BlockSpec((ts, HEAD_DIM), lambda i, h: (i, v_off + h)),
                  pl.BlockSpec((ts, HEAD_DIM), lambda i, h: (i, h))],
        out_specs=(aug, aug, pl.BlockSpec((VT_ROWS, ts), lambda i, h: (h, i))),
        compiler_params=_cparams("parallel", "parallel"),
        name="fox_prep",
    )(z, z, z, cexp)


def _fox_kernel(qa_ref, ka_ref, vt_ref, o_ref, sa_ref, sb_ref, m_ref, acc_ref, *, t):
    i = pl.program_id(1)
    qa = qa_ref[...]

    def scores(j):
        start = pl.multiple_of(j * t, t)
        return _dot_nt(ka_ref[pl.ds(start, t), :], qa)

    def block(s_ref, j, diagonal):
        s = s_ref[...]
        if diagonal:
            key = lax.broadcasted_iota(jnp.int32, (t, t), 0)
            qry = lax.broadcasted_iota(jnp.int32, (t, t), 1)
            s = jnp.where(key <= qry, s, -jnp.inf)
        m_old = m_ref[...]
        m_new = jnp.maximum(m_old, jnp.max(s, axis=0, keepdims=True))
        p = jnp.exp2(s - m_new).astype(BF16)
        vt = vt_ref[:, pl.ds(pl.multiple_of(j * t, t), t)]
        acc_ref[...] = jnp.exp2(m_old - m_new) * acc_ref[...] + _dot(vt, p)
        m_ref[...] = m_new

    m_ref[...] = jnp.full(m_ref.shape, -1e30, F32)
    acc_ref[...] = jnp.zeros_like(acc_ref)
    sa_ref[...] = scores(0)

    def body(jj, carry):
        j = 2 * jj
        sb_ref[...] = scores(j + 1)
        block(sa_ref, j, False)
        sa_ref[...] = scores(j + 2)
        block(sb_ref, j + 1, False)
        return carry

    lax.fori_loop(0, i // 2, body, 0)

    @pl.when(i % 2 == 1)
    def _():
        sb_ref[...] = scores(i)
        block(sa_ref, i - 1, False)
        block(sb_ref, i, True)

    @pl.when(i % 2 == 0)
    def _():
        block(sa_ref, i, True)

    o = acc_ref[0:HEAD_DIM, :] / acc_ref[HEAD_DIM:HEAD_DIM + 1, :]
    o_ref[...] = o.T.astype(o_ref.dtype)


def _fox(qa, ka, vt, heads):
    s = qa.shape[0]
    gw = heads * HEAD_DIM
    t = _tile(s, (512, 256))
    return pl.pallas_call(
        functools.partial(_fox_kernel, t=t),
        out_shape=jax.ShapeDtypeStruct((s, gw), BF16),
        grid=(heads, s // t),
        in_specs=[pl.BlockSpec((t, 2 * HEAD_DIM), lambda h, i: (i, h)),
                  pl.BlockSpec((s, 2 * HEAD_DIM), lambda h, i: (0, h)),
                  pl.BlockSpec((VT_ROWS, s), lambda h, i: (h, 0))],
        out_specs=pl.BlockSpec((t, HEAD_DIM), lambda h, i: (i, h)),
        scratch_shapes=[pltpu.VMEM((t, t), F32),
                        pltpu.VMEM((t, t), F32),
                        pltpu.VMEM((1, t), F32),
                        pltpu.VMEM((VT_ROWS, t), F32)],
        compiler_params=_cparams("parallel", "arbitrary"),
        name="fox_attention",
    )(qa, ka, vt)


def _merge_kernel(oa_ref, ob_ref, wa_ref, wb_ref, ga_ref, gb_ref, o_ref):
    ya = _dot(oa_ref[...], wa_ref[...])
    yb = _dot(ob_ref[...], wb_ref[...])
    y = _sigmoid(ga_ref[...].astype(F32)) * ya + _sigmoid(gb_ref[...].astype(F32)) * yb
    o_ref[...] = y.astype(o_ref.dtype)


def _merge(oa, ob, wa, wb, z, heads):
    s, gw = oa.shape
    d = wa.shape[1]
    tm = _tile(s, (1024, 512, 256, 128))
    tn = _tile(math.gcd(7 * gw, d), (512, 256, 128))
    ga_off = 7 * gw // tn
    gb_off = (7 * gw + d) // tn
    return pl.pallas_call(
        _merge_kernel,
        out_shape=jax.ShapeDtypeStruct((s, d), BF16),
        grid=(s // tm, d // tn),
        in_specs=[pl.BlockSpec((tm, gw), lambda i, j: (i, 0)),
                  pl.BlockSpec((tm, gw), lambda i, j: (i, 0)),
                  pl.BlockSpec((gw, tn), lambda i, j: (0, j)),
                  pl.BlockSpec((gw, tn), lambda i, j: (0, j)),
                  pl.BlockSpec((tm, tn), lambda i, j: (i, ga_off + j)),
                  pl.BlockSpec((tm, tn), lambda i, j: (i, gb_off + j))],
        out_specs=pl.BlockSpec((tm, tn), lambda i, j: (i, j)),
        compiler_params=_cparams("parallel", "arbitrary"),
        name="merge",
    )(oa, ob, wa, wb, z, z)


def _ffn_up_kernel(x_ref, wa_ref, wb_ref, ca_ref, cb_ref, o_ref, ta_ref, tb_ref, *, tm, cw):
    @pl.when(pl.program_id(1) == 0)
    def _():
        ta_ref[...] = jnp.zeros_like(ta_ref)
        tb_ref[...] = jnp.zeros_like(tb_ref)

    x = x_ref[...]

    def conv(u, tail_ref, c_ref, cols):
        ue = jnp.concatenate([tail_ref[:, cols], u], axis=0)
        c = c_ref[:, cols]
        y = c[FFN_CONV - 1:FFN_CONV, :] * u
        for kk in range(FFN_CONV - 1):
            off = TAIL - (FFN_CONV - 1) + kk
            y = y + c[kk:kk + 1, :] * ue[off:off + tm, :]
        tail_ref[:, cols] = u[tm - TAIL:tm, :]
        return y

    for c0 in range(0, o_ref.shape[1], cw):
        cols = slice(c0, c0 + cw)
        a = conv(_dot(x, wa_ref[:, cols]), ta_ref, ca_ref, cols)
        b = conv(_dot(x, wb_ref[:, cols]), tb_ref, cb_ref, cols)
        o_ref[:, cols] = (_silu(a) * b).astype(o_ref.dtype)


def _ffn_up(hn, w_up, conv_w):
    s, d = hn.shape
    dff = w_up.shape[1] // 2
    tm = _tile(s, (1024, 512, 256, 128, 64))
    tn = _tile(dff, (512, 256, 128))
    cw = _tile(tn, (256, 128))
    nb = dff // tn
    return pl.pallas_call(
        functools.partial(_ffn_up_kernel, tm=tm, cw=cw),
        out_shape=jax.ShapeDtypeStruct((s, dff), BF16),
        grid=(nb, s // tm),
        in_specs=[pl.BlockSpec((tm, d), lambda j, i: (i, 0)),
                  pl.BlockSpec((d, tn), lambda j, i: (0, j)),
                  pl.BlockSpec((d, tn), lambda j, i: (0, nb + j)),
                  pl.BlockSpec((FFN_CONV, tn), lambda j, i: (0, j)),
                  pl.BlockSpec((FFN_CONV, tn), lambda j, i: (0, nb + j))],
        out_specs=pl.BlockSpec((tm, tn), lambda j, i: (i, j)),
        scratch_shapes=[pltpu.VMEM((TAIL, tn), F32), pltpu.VMEM((TAIL, tn), F32)],
        compiler_params=_cparams("parallel", "arbitrary"),
        name="ffn_up",
    )(hn, w_up, w_up, conv_w, conv_w)


def _mixer_weights(w_in, heads):
    d = w_in.shape[0]
    gw = heads * HEAD_DIM
    o_small = 4 * gw
    o_fqkv = o_small + 2 * heads
    o_ff = o_fqkv + 3 * gw
    o_gates = o_ff + heads
    w_main = jnp.concatenate(
        [w_in[:, 0:o_small], w_in[:, o_fqkv:o_ff], w_in[:, o_gates:o_gates + 2 * d]], axis=1).astype(BF16)
    w_small = jnp.concatenate(
        [w_in[:, o_small:o_fqkv], w_in[:, o_ff:o_gates],
         jnp.zeros((d, LANES - 3 * heads), w_in.dtype)], axis=1).astype(BF16)
    return w_main, w_small


def _gate_params(a_log, dt_bias, f_bias, heads):
    prm = jnp.zeros((8, LANES), F32)
    prm = prm.at[0, 0:heads].set(a_log.astype(F32))
    prm = prm.at[1, 0:heads].set(dt_bias.astype(F32))
    prm = prm.at[2, 2 * heads:3 * heads].set(f_bias.astype(F32))
    return prm


def _selector(heads):
    gw = heads * HEAD_DIM
    src = lax.broadcasted_iota(jnp.int32, (LANES, 3 * gw), 0)
    dst = lax.broadcasted_iota(jnp.int32, (LANES, 3 * gw), 1)
    want = (dst // gw) * heads + (dst % gw) // HEAD_DIM
    return jnp.where(src == want, 1.0, 0.0).astype(BF16)


def kernel(x, mix_norm, w_in, gdn_conv, gdn_a_log, gdn_dt_bias, gdn_norm, fox_f_bias,
           w_proj_a, w_proj_b, w_out, ffn_norm, w_up, ffn_conv, w_down, final_norm):
    batch, seq, d = x.shape
    depth = w_in.shape[0]
    heads = d // (2 * HEAD_DIM)
    assert 3 * heads <= LANES and seq % CHUNK == 0
    sel = _selector(heads)
    outs = []
    for bi in range(batch):
        h = x[bi].astype(F32)
        for l in range(depth):
            w_main, w_small = _mixer_weights(w_in[l], heads)
            hn = _rmsnorm(h, mix_norm[l], BF16)
            z = _matmul(hn, w_main, BF16, "in_proj")
            prm = _gate_params(gdn_a_log[l], gdn_dt_bias[l], fox_f_bias[l], heads)
            gexp, bexp, cexp, rows = _gates(hn, w_small, prm, sel, heads)
            kw, b, qeff, o0, gl = _gdn_prep(z, gdn_conv[l].astype(F32), gexp, bexp, rows, heads)
            oa = _gdn_scan(kw, b, qeff, o0, gl, z, gdn_norm[l].astype(F32), heads)
            qa, ka, vt = _fox_prep(z, cexp, heads)
            ob = _fox(qa, ka, vt, heads)
            y = _merge(oa, ob, w_proj_a[l].astype(BF16), w_proj_b[l].astype(BF16), z, heads)
            h = _matmul_residual(y, w_out[l].astype(BF16), h, "out_proj")
            hn = _rmsnorm(h, ffn_norm[l], BF16)
            act = _ffn_up(hn, w_up[l].astype(BF16), ffn_conv[l].astype(F32))
            h = _matmul_residual(act, w_down[l].astype(BF16), h, "ffn_down")
        outs.append(_rmsnorm(h, final_norm, x.dtype))
    return jnp.stack(outs, axis=0)
```

```python
import functools
import math

import jax
import jax.numpy as jnp
from jax import lax
from jax.experimental import pallas as pl
from jax.experimental.pallas import tpu as pltpu

F32 = jnp.float32
BF16 = jnp.bfloat16

HEAD_DIM = 128
CHUNK = 64
SHORT_CONV = 4
FFN_CONV = 3
NORM_EPS = 1e-6
LOG2E = 1.4426950408889634
LANES = 128
VT_ROWS = HEAD_DIM + 16
TAIL = 8
HALO = 16
VMEM_LIMIT = 56 * 1024 * 1024


def _cparams(*sem):
    return pltpu.CompilerParams(dimension_semantics=sem, vmem_limit_bytes=VMEM_LIMIT)


def _tile(n, prefs):
    for t in prefs:
        if n % t == 0:
            return t
    return n


def _dot(a, b):
    return jnp.dot(a, b, preferred_element_type=F32)


def _dot_nt(a, b):
    return lax.dot_general(a, b, (((1,), (1,)), ((), ())), preferred_element_type=F32)


def _dot_tn(a, b):
    return lax.dot_general(a, b, (((0,), (0,)), ((), ())), preferred_element_type=F32)


def _split3(x):
    x1 = x.astype(BF16)
    r1 = x - x1.astype(F32)
    x2 = r1.astype(BF16)
    r2 = r1 - x2.astype(F32)
    return x1, x2, r2.astype(BF16)


def _dot_exact_rhs(sel, x):
    x1, x2, x3 = _split3(x)
    return _dot(sel, x1) + _dot(sel, x2) + _dot(sel, x3)


def _dot_exact_lhs(x, sel):
    x1, x2, x3 = _split3(x)
    return _dot(x1, sel) + _dot(x2, sel) + _dot(x3, sel)


def _sigmoid(x):
    return 1.0 / (1.0 + jnp.exp(-x))


def _silu(x):
    return x * _sigmoid(x)


def _softplus(x):
    return jnp.maximum(x, 0.0) + jnp.log1p(jnp.exp(-jnp.abs(x)))


def _rmsnorm_kernel(x_ref, g_ref, o_ref):
    x = x_ref[...]
    r = lax.rsqrt(jnp.mean(x * x, axis=-1, keepdims=True) + NORM_EPS)
    o_ref[...] = ((x * r) * g_ref[...]).astype(o_ref.dtype)


def _rmsnorm(x, g, out_dtype):
    s, d = x.shape
    tm = _tile(s, (256, 128, 64, 32, 16))
    return pl.pallas_call(
        _rmsnorm_kernel,
        out_shape=jax.ShapeDtypeStruct((s, d), out_dtype),
        grid=(s // tm,),
        in_specs=[pl.BlockSpec((tm, d), lambda i: (i, 0)),
                  pl.BlockSpec((1, d), lambda i: (0, 0))],
        out_specs=pl.BlockSpec((tm, d), lambda i: (i, 0)),
        compiler_params=_cparams("parallel"),
        name="rmsnorm",
    )(x, g.reshape(1, d))


def _mm_kernel(x_ref, w_ref, o_ref):
    o_ref[...] = _dot(x_ref[...], w_ref[...]).astype(o_ref.dtype)


def _matmul(x, w, out_dtype, name):
    m, k = x.shape
    n = w.shape[1]
    tm = _tile(m, (1024, 512, 256, 128))
    tn = _tile(n, (1024, 512, 256, 128))
    return pl.pallas_call(
        _mm_kernel,
        out_shape=jax.ShapeDtypeStruct((m, n), out_dtype),
        grid=(m // tm, n // tn),
        in_specs=[pl.BlockSpec((tm, k), lambda i, j: (i, 0)),
                  pl.BlockSpec((k, tn), lambda i, j: (0, j))],
        out_specs=pl.BlockSpec((tm, tn), lambda i, j: (i, j)),
        compiler_params=_cparams("parallel", "arbitrary"),
        name=name,
    )(x, w)


def _mm_res_kernel(x_ref, w_ref, r_ref, o_ref):
    o_ref[...] = r_ref[...] + _dot(x_ref[...], w_ref[...])


def _matmul_residual(x, w, res, name):
    m, k = x.shape
    n = w.shape[1]
    tm = _tile(m, tuple(t for t in (1024, 512, 256, 128) if t * k <= 4 * 1024 * 1024))
    tn = _tile(n, (512, 256, 128))
    return pl.pallas_call(
        _mm_res_kernel,
        out_shape=jax.ShapeDtypeStruct((m, n), F32),
        grid=(m // tm, n // tn),
        in_specs=[pl.BlockSpec((tm, k), lambda i, j: (i, 0)),
                  pl.BlockSpec((k, tn), lambda i, j: (0, j)),
                  pl.BlockSpec((tm, tn), lambda i, j: (i, j))],
        out_specs=pl.BlockSpec((tm, tn), lambda i, j: (i, j)),
        compiler_params=_cparams("parallel", "arbitrary"),
        name=name,
    )(x, w, res)


def _gates_kernel(hn_ref, ws_ref, prm_ref, sel_ref, gexp_ref, bexp_ref, cexp_ref, rows_ref,
                  carry_ref, *, heads, tm):
    i = pl.program_id(0)

    @pl.when(i == 0)
    def _():
        carry_ref[...] = jnp.zeros_like(carry_ref)

    zs = _dot(hn_ref[...], ws_ref[...])
    lane = lax.broadcasted_iota(jnp.int32, zs.shape, 1)
    in_a = lane < heads
    in_b = (lane >= heads) & (lane < 2 * heads)
    in_f = (lane >= 2 * heads) & (lane < 3 * heads)
    a_log = prm_ref[0:1, :]
    dt_b = prm_ref[1:2, :]
    f_b = prm_ref[2:3, :]
    log_alpha = jnp.where(in_a, -jnp.exp(a_log) * _softplus(zs + dt_b), 0.0)
    beta = jnp.where(in_b, _sigmoid(zs), 0.0)
    log_f = jnp.where(in_f, -_softplus(-(zs + f_b)), 0.0)

    row = lax.broadcasted_iota(jnp.int32, (tm, tm), 0)
    col = lax.broadcasted_iota(jnp.int32, (tm, tm), 1)
    tri = col <= row
    tri_all = jnp.where(tri, 1.0, 0.0).astype(BF16)
    tri_chunk = jnp.where(tri & ((row // CHUNK) == (col // CHUNK)), 1.0, 0.0).astype(BF16)

    g_cum = _dot_exact_rhs(tri_chunk, log_alpha)
    c_cum = _dot_exact_rhs(tri_all, log_f) + jnp.where(in_f, carry_ref[...], 0.0)
    carry_ref[...] = c_cum[tm - 1:tm, :]

    packed = g_cum + beta + c_cum
    gw = gexp_ref.shape[1]
    gexp_ref[...] = _dot_exact_lhs(packed, sel_ref[:, 0:gw])
    bexp_ref[...] = _dot_exact_lhs(packed, sel_ref[:, gw:2 * gw])
    cexp_ref[...] = _dot_exact_lhs(packed, sel_ref[:, 2 * gw:3 * gw])
    rows_ref[...] = packed.T


def _gates(hn, w_small, prm, sel, heads):
    s, d = hn.shape
    gw = heads * HEAD_DIM
    tm = _tile(s, (256, 128, 64))
    exp_spec = pl.BlockSpec((tm, gw), lambda i: (i, 0))
    return pl.pallas_call(
        functools.partial(_gates_kernel, heads=heads, tm=tm),
        out_shape=(jax.ShapeDtypeStruct((s, gw), F32),) * 3 + (jax.ShapeDtypeStruct((LANES, s), F32),),
        grid=(s // tm,),
        in_specs=[pl.BlockSpec((tm, d), lambda i: (i, 0)),
                  pl.BlockSpec((d, LANES), lambda i: (0, 0)),
                  pl.BlockSpec((8, LANES), lambda i: (0, 0)),
                  pl.BlockSpec((LANES, 3 * gw), lambda i: (0, 0))],
        out_specs=(exp_spec, exp_spec, exp_spec, pl.BlockSpec((LANES, tm), lambda i: (0, i))),
        scratch_shapes=[pltpu.VMEM((1, LANES), F32)],
        compiler_params=_cparams("arbitrary"),
        name="gates",
    )(hn, w_small, prm, sel)


def _gdn_prep_kernel(q_ref, k_ref, v_ref, qh_ref, kh_ref, vh_ref, cq_ref, ck_ref, cv_ref,
                     gexp_ref, bexp_ref, rows_ref,
                     kw_ref, b_ref, qeff_ref, o0_ref, gl_ref, *, tm):
    i = pl.program_id(0)
    h = pl.program_id(1)
    first = i == 0

    def conv_silu(x_ref, halo_ref, w_ref):
        x = x_ref[...].astype(F32)
        halo = jnp.where(first, 0.0, halo_ref[...].astype(F32))
        xe = jnp.concatenate([halo, x], axis=0)
        w = w_ref[...]
        y = w[SHORT_CONV - 1:SHORT_CONV, :] * x
        for kk in range(SHORT_CONV - 1):
            off = HALO - (SHORT_CONV - 1) + kk
            y = y + w[kk:kk + 1, :] * xe[off:off + tm, :]
        return _silu(y)

    q = conv_silu(q_ref, qh_ref, cq_ref)
    k = conv_silu(k_ref, kh_ref, ck_ref)
    v = conv_silu(v_ref, vh_ref, cv_ref)
    q = q * (lax.rsqrt(jnp.sum(q * q, axis=-1, keepdims=True) + NORM_EPS) * (HEAD_DIM ** -0.5))
    k = k * lax.rsqrt(jnp.sum(k * k, axis=-1, keepdims=True) + NORM_EPS)

    g_all = gexp_ref[...]
    bt_all = bexp_ref[...]
    g_rows = rows_ref[pl.ds(h % 8, 1), :]

    ii = lax.broadcasted_iota(jnp.int32, (CHUNK, CHUNK), 0)
    jj = lax.broadcasted_iota(jnp.int32, (CHUNK, CHUNK), 1)
    causal = jj <= ii
    strict = jj < ii

    sls = [slice(c * CHUNK, (c + 1) * CHUNK) for c in range(tm // CHUNK)]
    gs = [g_all[sl] for sl in sls]
    bts = [bt_all[sl] for sl in sls]
    egs = [jnp.exp(g) for g in gs]
    g_lasts = [g[CHUNK - 1:CHUNK, :] for g in gs]
    decays = [jnp.exp(jnp.where(causal, g[:, 0:CHUNK] - g_rows[:, sl], -jnp.inf)) for g, sl in zip(gs, sls)]
    qkks = [_dot_nt(jnp.concatenate([q[sl], k[sl]], axis=0).astype(BF16), k[sl].astype(BF16)) for sl in sls]
    qks = [(qkk[0:CHUNK] * d).astype(BF16) for qkk, d in zip(qkks, decays)]
    ps = [jnp.where(strict, -(bt[:, 0:CHUNK] * qkk[CHUNK:2 * CHUNK] * d), 0.0).astype(BF16)
          for bt, qkk, d in zip(bts, qkks, decays)]
    xs = [jnp.concatenate([(bt * eg) * k[sl], bt * v[sl]], axis=1) for bt, eg, sl in zip(bts, egs, sls)]
    levels = CHUNK.bit_length() - 1
    for lev in range(levels):
        xs = [x + _dot(p, x.astype(BF16)) for p, x in zip(ps, xs)]
        if lev + 1 < levels:
            ps = [_dot(p, p).astype(BF16) for p in ps]
    wubs = [x.astype(BF16) for x in xs]
    k_decs = [(k[sl] * jnp.exp(gl - g)).astype(BF16) for sl, gl, g in zip(sls, g_lasts, gs)]
    kwbs = [_dot_tn(kd, wub) for kd, wub in zip(k_decs, wubs)]
    qos = [_dot(qk, wub) for qk, wub in zip(qks, wubs)]
    for c, sl in enumerate(sls):
        kw_ref[c] = kwbs[c][:, 0:HEAD_DIM].astype(BF16)
        b_ref[c] = kwbs[c][:, HEAD_DIM:2 * HEAD_DIM]
        qeff_ref[sl, :] = (q[sl] * egs[c] - qos[c][:, 0:HEAD_DIM]).astype(BF16)
        o0_ref[sl, :] = qos[c][:, HEAD_DIM:2 * HEAD_DIM]
        gl_ref[c] = jnp.exp(g_lasts[c])


def _gdn_prep(z, conv_w, gexp, bexp, rows, heads):
    s = z.shape[0]
    gw = heads * HEAD_DIM
    nc = s // CHUNK
    tm = _tile(s, (512, 256, 128, 64))
    cpt = tm // CHUNK
    hb = tm // HALO

    def col(off):
        return lambda i, h: (i, off + h)

    def halo(off):
        return lambda i, h: (jnp.maximum(i * hb - 1, 0), off + h)

    def wcol(off):
        return lambda i, h: (0, off + h)

    blk = pl.BlockSpec((tm, HEAD_DIM), col(0))
    return pl.pallas_call(
        functools.partial(_gdn_prep_kernel, tm=tm),
        out_shape=(jax.ShapeDtypeStruct((nc, HEAD_DIM, gw), BF16),
                   jax.ShapeDtypeStruct((nc, HEAD_DIM, gw), F32),
                   jax.ShapeDtypeStruct((s, gw), BF16),
                   jax.ShapeDtypeStruct((s, gw), F32),
                   jax.ShapeDtypeStruct((nc, 1, gw), F32)),
        grid=(s // tm, heads),
        in_specs=[pl.BlockSpec((tm, HEAD_DIM), col(0)),
                  pl.BlockSpec((tm, HEAD_DIM), col(heads)),
                  pl.BlockSpec((tm, HEAD_DIM), col(2 * heads)),
                  pl.BlockSpec((HALO, HEAD_DIM), halo(0)),
                  pl.BlockSpec((HALO, HEAD_DIM), halo(heads)),
                  pl.BlockSpec((HALO, HEAD_DIM), halo(2 * heads)),
                  pl.BlockSpec((SHORT_CONV, HEAD_DIM), wcol(0)),
                  pl.BlockSpec((SHORT_CONV, HEAD_DIM), wcol(heads)),
                  pl.BlockSpec((SHORT_CONV, HEAD_DIM), wcol(2 * heads)),
                  blk, blk,
                  pl.BlockSpec((8, tm), lambda i, h: (h // 8, i))],
        out_specs=(pl.BlockSpec((cpt, HEAD_DIM, HEAD_DIM), lambda i, h: (i, 0, h)),
                   pl.BlockSpec((cpt, HEAD_DIM, HEAD_DIM), lambda i, h: (i, 0, h)),
                   blk, blk,
                   pl.BlockSpec((cpt, 1, HEAD_DIM), lambda i, h: (i, 0, h))),
        compiler_params=_cparams("parallel", "parallel"),
        name="gdn_prep",
    )(z, z, z, z, z, z, conv_w, conv_w, conv_w, gexp, bexp, rows)


def _gdn_scan_kernel(kw_ref, b_ref, qeff_ref, o0_ref, gl_ref, gate_ref, gn_ref, o_ref, s_ref,
                     *, cpt, group):
    @pl.when(pl.program_id(1) == 0)
    def _():
        s_ref[...] = jnp.zeros_like(s_ref)

    gn = gn_ref[...]
    for c in range(cpt):
        rows = slice(c * CHUNK, (c + 1) * CHUNK)
        for g in range(group):
            cols = slice(g * HEAD_DIM, (g + 1) * HEAD_DIM)
            st = s_ref[g]
            sb = st.astype(BF16)
            o = _dot(qeff_ref[rows, cols], sb) + o0_ref[rows, cols]
            s_ref[g] = gl_ref[c, :, cols] * st + (b_ref[c, :, cols] - _dot(kw_ref[c, :, cols], sb))
            r = lax.rsqrt(jnp.mean(o * o, axis=-1, keepdims=True) + NORM_EPS)
            o_ref[rows, cols] = ((o * r) * gn * _silu(gate_ref[rows, cols].astype(F32))).astype(o_ref.dtype)


def _gdn_scan(kw, b, qeff, o0, gl, z, gn, heads):
    s = qeff.shape[0]
    gw = heads * HEAD_DIM
    tm = _tile(s, (256, 128, 64))
    cpt = tm // CHUNK
    group = 8 if heads % 8 == 0 else heads
    gwid = group * HEAD_DIM
    gate_off = 3 * gw // gwid
    mat = pl.BlockSpec((cpt, HEAD_DIM, gwid), lambda hg, i: (i, 0, hg))
    blk = pl.BlockSpec((tm, gwid), lambda hg, i: (i, hg))
    return pl.pallas_call(
        functools.partial(_gdn_scan_kernel, cpt=cpt, group=group),
        out_shape=jax.ShapeDtypeStruct((s, gw), BF16),
        grid=(heads // group, s // tm),
        in_specs=[mat, mat, blk, blk,
                  pl.BlockSpec((cpt, 1, gwid), lambda hg, i: (i, 0, hg)),
                  pl.BlockSpec((tm, gwid), lambda hg, i: (i, gate_off + hg)),
                  pl.BlockSpec((1, HEAD_DIM), lambda hg, i: (0, 0))],
        out_specs=blk,
        scratch_shapes=[pltpu.VMEM((group, HEAD_DIM, HEAD_DIM), F32)],
        compiler_params=_cparams("parallel", "arbitrary"),
        name="gdn_scan",
    )(kw, b, qeff, o0, gl, z, gn.reshape(1, HEAD_DIM))


def _fox_prep_kernel(q_ref, k_ref, v_ref, c_ref, qa_ref, ka_ref, vt_ref):
    lane = lax.broadcasted_iota(jnp.int32, c_ref.shape, 1)
    qs = (q_ref[...].astype(F32) * (HEAD_DIM ** -0.5 * LOG2E)).astype(BF16)
    qa_ref[...] = jnp.concatenate([qs, jnp.where(lane < 3, 1.0, 0.0).astype(BF16)], axis=1)
    c1, c2, c3 = _split3(c_ref[...] * (-LOG2E))
    zero = jnp.zeros_like(c1)
    pieces = jnp.where(lane == 0, c1, jnp.where(lane == 1, c2, jnp.where(lane == 2, c3, zero)))
    ka_ref[...] = jnp.concatenate([k_ref[...], pieces], axis=1)
    vt_ref[0:HEAD_DIM, :] = v_ref[...].astype(F32).T.astype(BF16)
    vt_ref[HEAD_DIM:VT_ROWS, :] = jnp.ones((VT_ROWS - HEAD_DIM, vt_ref.shape[1]), BF16)


def _fox_prep(z, cexp, heads):
    s = z.shape[0]
    gw = heads * HEAD_DIM
    ts = _tile(s, (2048, 1024, 512, 256, 128))
    q_off, k_off, v_off = 4 * heads, 5 * heads, 6 * heads
    aug = pl.BlockSpec((ts, 2 * HEAD_DIM), lambda i, h: (i, h))
    return pl.pallas_call(
        _fox_prep_kernel,
        out_shape=(jax.ShapeDtypeStruct((s, 2 * gw), BF16),
                   jax.ShapeDtypeStruct((s, 2 * gw), BF16),
                   jax.ShapeDtypeStruct((heads * VT_ROWS, s), BF16)),
        grid=(s // ts, heads),
        in_specs=[pl.BlockSpec((ts, HEAD_DIM), lambda i, h: (i, q_off + h)),
                  pl.BlockSpec((ts, HEAD_DIM), lambda i, h: (i, k_off + h)),
                  pl.BlockSpec((ts, HEAD_DIM), lambda i, h: (i, v_off + h)),
                  pl.BlockSpec((ts, HEAD_DIM), lambda i, h: (i, h))],
        out_specs=(aug, aug, pl.BlockSpec((VT_ROWS, ts), lambda i, h: (h, i))),
        compiler_params=_cparams("parallel", "parallel"),
        name="fox_prep",
    )(z, z, z, cexp)


def _fox_kernel(qa_ref, ka_ref, vt_ref, o_ref, sa_ref, sb_ref, m_ref, acc_ref, *, tq):
    i = pl.program_id(1)
    tk = tq // 2
    qa = qa_ref[...]
    qw = min(tq, 2 * LANES)

    def scores(j):
        start = pl.multiple_of(j * tk, tk)
        return _dot_nt(ka_ref[pl.ds(start, tk), :], qa)

    def block(s_ref, j, diag_offset):
        vt = vt_ref[:, pl.ds(pl.multiple_of(j * tk, tk), tk)]
        for c0 in range(0, tq, qw):
            cols = slice(c0, c0 + qw)
            s = s_ref[:, cols]
            if diag_offset is not None:
                key = lax.broadcasted_iota(jnp.int32, (tk, qw), 0) + diag_offset
                qry = lax.broadcasted_iota(jnp.int32, (tk, qw), 1) + c0
                s = jnp.where(key <= qry, s, -jnp.inf)
            m_old = m_ref[:, cols]
            m_new = jnp.maximum(m_old, jnp.max(s, axis=0, keepdims=True))
            p = jnp.exp2(s - m_new).astype(BF16)
            acc_ref[:, cols] = jnp.exp2(m_old - m_new) * acc_ref[:, cols] + _dot(vt, p)
            m_ref[:, cols] = m_new

    m_ref[...] = jnp.full(m_ref.shape, -1e30, F32)
    acc_ref[...] = jnp.zeros_like(acc_ref)
    sa_ref[...] = scores(0)

    def pair(j):
        sb_ref[...] = scores(j + 1)
        block(sa_ref, j, None)
        sa_ref[...] = scores(j + 2)
        block(sb_ref, j + 1, None)

    def body(jj, carry):
        pair(4 * jj)
        pair(4 * jj + 2)
        return carry

    lax.fori_loop(0, i // 2, body, 0)

    @pl.when(i % 2 == 1)
    def _():
        pair(2 * i - 2)

    sb_ref[...] = scores(2 * i + 1)
    block(sa_ref, 2 * i, 0)
    block(sb_ref, 2 * i + 1, tk)

    o = acc_ref[0:HEAD_DIM, :] / acc_ref[HEAD_DIM:HEAD_DIM + 1, :]
    o_ref[...] = o.T.astype(o_ref.dtype)


def _fox(qa, ka, vt, heads):
    s = qa.shape[0]
    gw = heads * HEAD_DIM
    tq = _tile(s, (1024, 512, 256))
    return pl.pallas_call(
        functools.partial(_fox_kernel, tq=tq),
        out_shape=jax.ShapeDtypeStruct((s, gw), BF16),
        grid=(heads, s // tq),
        in_specs=[pl.BlockSpec((tq, 2 * HEAD_DIM), lambda h, i: (i, h)),
                  pl.BlockSpec((s, 2 * HEAD_DIM), lambda h, i: (0, h)),
                  pl.BlockSpec((VT_ROWS, s), lambda h, i: (h, 0))],
        out_specs=pl.BlockSpec((tq, HEAD_DIM), lambda h, i: (i, h)),
        scratch_shapes=[pltpu.VMEM((tq // 2, tq), F32),
                        pltpu.VMEM((tq // 2, tq), F32),
                        pltpu.VMEM((1, tq), F32),
                        pltpu.VMEM((VT_ROWS, tq), F32)],
        compiler_params=_cparams("parallel", "arbitrary"),
        name="fox_attention",
    )(qa, ka, vt)


def _merge_kernel(oa_ref, ob_ref, wa_ref, wb_ref, ga_ref, gb_ref, o_ref):
    ya = _dot(oa_ref[...], wa_ref[...])
    yb = _dot(ob_ref[...], wb_ref[...])
    y = _sigmoid(ga_ref[...].astype(F32)) * ya + _sigmoid(gb_ref[...].astype(F32)) * yb
    o_ref[...] = y.astype(o_ref.dtype)


def _merge(oa, ob, wa, wb, z, heads):
    s, gw = oa.shape
    d = wa.shape[1]
    tm = _tile(s, (1024, 512, 256, 128))
    tn = _tile(math.gcd(7 * gw, d), (512, 256, 128))
    ga_off = 7 * gw // tn
    gb_off = (7 * gw + d) // tn
    return pl.pallas_call(
        _merge_kernel,
        out_shape=jax.ShapeDtypeStruct((s, d), BF16),
        grid=(s // tm, d // tn),
        in_specs=[pl.BlockSpec((tm, gw), lambda i, j: (i, 0)),
                  pl.BlockSpec((tm, gw), lambda i, j: (i, 0)),
                  pl.BlockSpec((gw, tn), lambda i, j: (0, j)),
                  pl.BlockSpec((gw, tn), lambda i, j: (0, j)),
                  pl.BlockSpec((tm, tn), lambda i, j: (i, ga_off + j)),
                  pl.BlockSpec((tm, tn), lambda i, j: (i, gb_off + j))],
        out_specs=pl.BlockSpec((tm, tn), lambda i, j: (i, j)),
        compiler_params=_cparams("parallel", "arbitrary"),
        name="merge",
    )(oa, ob, wa, wb, z, z)


def _ffn_up_kernel(x_ref, wa_ref, wb_ref, ca_ref, cb_ref, o_ref, ta_ref, tb_ref, *, tm, cw):
    @pl.when(pl.program_id(1) == 0)
    def _():
        ta_ref[...] = jnp.zeros_like(ta_ref)
        tb_ref[...] = jnp.zeros_like(tb_ref)

    x = x_ref[...]

    def conv(u, tail_ref, c_ref, cols):
        ue = jnp.concatenate([tail_ref[:, cols], u], axis=0)
        c = c_ref[:, cols]
        y = c[FFN_CONV - 1:FFN_CONV, :] * u
        for kk in range(FFN_CONV - 1):
            off = TAIL - (FFN_CONV - 1) + kk
            y = y + c[kk:kk + 1, :] * ue[off:off + tm, :]
        tail_ref[:, cols] = u[tm - TAIL:tm, :]
        return y

    for c0 in range(0, o_ref.shape[1], cw):
        cols = slice(c0, c0 + cw)
        a = conv(_dot(x, wa_ref[:, cols]), ta_ref, ca_ref, cols)
        b = conv(_dot(x, wb_ref[:, cols]), tb_ref, cb_ref, cols)
        o_ref[:, cols] = (_silu(a) * b).astype(o_ref.dtype)


def _ffn_up(hn, w_up, conv_w):
    s, d = hn.shape
    dff = w_up.shape[1] // 2
    tm = _tile(s, (1024, 512, 256, 128, 64))
    tn = _tile(dff, (512, 256, 128))
    cw = _tile(tn, (256, 128))
    nb = dff // tn
    return pl.pallas_call(
        functools.partial(_ffn_up_kernel, tm=tm, cw=cw),
        out_shape=jax.ShapeDtypeStruct((s, dff), BF16),
        grid=(nb, s // tm),
        in_specs=[pl.BlockSpec((tm, d), lambda j, i: (i, 0)),
                  pl.BlockSpec((d, tn), lambda j, i: (0, j)),
                  pl.BlockSpec((d, tn), lambda j, i: (0, nb + j)),
                  pl.BlockSpec((FFN_CONV, tn), lambda j, i: (0, j)),
                  pl.BlockSpec((FFN_CONV, tn), lambda j, i: (0, nb + j))],
        out_specs=pl.BlockSpec((tm, tn), lambda j, i: (i, j)),
        scratch_shapes=[pltpu.VMEM((TAIL, tn), F32), pltpu.VMEM((TAIL, tn), F32)],
        compiler_params=_cparams("parallel", "arbitrary"),
        name="ffn_up",
    )(hn, w_up, w_up, conv_w, conv_w)


def _mixer_weights(w_in, heads):
    d = w_in.shape[0]
    gw = heads * HEAD_DIM
    o_small = 4 * gw
    o_fqkv = o_small + 2 * heads
    o_ff = o_fqkv + 3 * gw
    o_gates = o_ff + heads
    w_main = jnp.concatenate(
        [w_in[:, 0:o_small], w_in[:, o_fqkv:o_ff], w_in[:, o_gates:o_gates + 2 * d]], axis=1).astype(BF16)
    w_small = jnp.concatenate(
        [w_in[:, o_small:o_fqkv], w_in[:, o_ff:o_gates],
         jnp.zeros((d, LANES - 3 * heads), w_in.dtype)], axis=1).astype(BF16)
    return w_main, w_small


def _gate_params(a_log, dt_bias, f_bias, heads):
    prm = jnp.zeros((8, LANES), F32)
    prm = prm.at[0, 0:heads].set(a_log.astype(F32))
    prm = prm.at[1, 0:heads].set(dt_bias.astype(F32))
    prm = prm.at[2, 2 * heads:3 * heads].set(f_bias.astype(F32))
    return prm


def _selector(heads):
    gw = heads * HEAD_DIM
    src = lax.broadcasted_iota(jnp.int32, (LANES, 3 * gw), 0)
    dst = lax.broadcasted_iota(jnp.int32, (LANES, 3 * gw), 1)
    want = (dst // gw) * heads + (dst % gw) // HEAD_DIM
    return jnp.where(src == want, 1.0, 0.0).astype(BF16)


def kernel(x, mix_norm, w_in, gdn_conv, gdn_a_log, gdn_dt_bias, gdn_norm, fox_f_bias,
           w_proj_a, w_proj_b, w_out, ffn_norm, w_up, ffn_conv, w_down, final_norm):
    batch, seq, d = x.shape
    depth = w_in.shape[0]
    heads = d // (2 * HEAD_DIM)
    assert 3 * heads <= LANES and seq % CHUNK == 0
    sel = _selector(heads)
    outs = []
    for bi in range(batch):
        h = x[bi].astype(F32)
        for l in range(depth):
            w_main, w_small = _mixer_weights(w_in[l], heads)
            hn = _rmsnorm(h, mix_norm[l], BF16)
            z = _matmul(hn, w_main, BF16, "in_proj")
            prm = _gate_params(gdn_a_log[l], gdn_dt_bias[l], fox_f_bias[l], heads)
            gexp, bexp, cexp, rows = _gates(hn, w_small, prm, sel, heads)
            kw, b, qeff, o0, gl = _gdn_prep(z, gdn_conv[l].astype(F32), gexp, bexp, rows, heads)
            oa = _gdn_scan(kw, b, qeff, o0, gl, z, gdn_norm[l].astype(F32), heads)
            qa, ka, vt = _fox_prep(z, cexp, heads)
            ob = _fox(qa, ka, vt, heads)
            y = _merge(oa, ob, w_proj_a[l].astype(BF16), w_proj_b[l].astype(BF16), z, heads)
            h = _matmul_residual(y, w_out[l].astype(BF16), h, "out_proj")
            hn = _rmsnorm(h, ffn_norm[l], BF16)
            act = _ffn_up(hn, w_up[l].astype(BF16), ffn_conv[l].astype(F32))
            h = _matmul_residual(act, w_down[l].astype(BF16), h, "ffn_down")
        outs.append(_rmsnorm(h, final_norm, x.dtype))
    return jnp.stack(outs, axis=0)
```

```python
import functools
import math

import jax
import jax.numpy as jnp
from jax import lax
from jax.experimental import pallas as pl
from jax.experimental.pallas import tpu as pltpu

F32 = jnp.float32
BF16 = jnp.bfloat16

HEAD_DIM = 128
CHUNK = 64
SHORT_CONV = 4
FFN_CONV = 3
NORM_EPS = 1e-6
LOG2E = 1.4426950408889634
LANES = 128
VT_ROWS = HEAD_DIM + 16
TAIL = 8
HALO = 16
VMEM_LIMIT = 56 * 1024 * 1024


def _cparams(*sem):
    return pltpu.CompilerParams(dimension_semantics=sem, vmem_limit_bytes=VMEM_LIMIT)


def _tile(n, prefs):
    for t in prefs:
        if n % t == 0:
            return t
    return n


def _dot(a, b):
    return jnp.dot(a, b, preferred_element_type=F32)


def _dot_nt(a, b):
    return lax.dot_general(a, b, (((1,), (1,)), ((), ())), preferred_element_type=F32)


def _dot_tn(a, b):
    return lax.dot_general(a, b, (((0,), (0,)), ((), ())), preferred_element_type=F32)


def _split3(x):
    x1 = x.astype(BF16)
    r1 = x - x1.astype(F32)
    x2 = r1.astype(BF16)
    r2 = r1 - x2.astype(F32)
    return x1, x2, r2.astype(BF16)


def _dot_exact_rhs(sel, x):
    x1, x2, x3 = _split3(x)
    return _dot(sel, x1) + _dot(sel, x2) + _dot(sel, x3)


def _dot_exact_lhs(x, sel):
    x1, x2, x3 = _split3(x)
    return _dot(x1, sel) + _dot(x2, sel) + _dot(x3, sel)


def _sigmoid(x):
    return 1.0 / (1.0 + jnp.exp(-x))


def _silu(x):
    return x * _sigmoid(x)


def _softplus(x):
    return jnp.maximum(x, 0.0) + jnp.log1p(jnp.exp(-jnp.abs(x)))


def _rmsnorm_kernel(x_ref, g_ref, o_ref):
    x = x_ref[...]
    r = lax.rsqrt(jnp.mean(x * x, axis=-1, keepdims=True) + NORM_EPS)
    o_ref[...] = ((x * r) * g_ref[...]).astype(o_ref.dtype)


def _rmsnorm(x, g, out_dtype):
    s, d = x.shape
    tm = _tile(s, (256, 128, 64, 32, 16))
    return pl.pallas_call(
        _rmsnorm_kernel,
        out_shape=jax.ShapeDtypeStruct((s, d), out_dtype),
        grid=(s // tm,),
        in_specs=[pl.BlockSpec((tm, d), lambda i: (i, 0)),
                  pl.BlockSpec((1, d), lambda i: (0, 0))],
        out_specs=pl.BlockSpec((tm, d), lambda i: (i, 0)),
        compiler_params=_cparams("parallel"),
        name="rmsnorm",
    )(x, g.reshape(1, d))


def _mm_kernel(x_ref, w_ref, o_ref):
    o_ref[...] = _dot(x_ref[...], w_ref[...]).astype(o_ref.dtype)


def _matmul(x, w, l, out_dtype, name):
    m, k = x.shape
    n = w.shape[2]
    tm = _tile(m, (1024, 512, 256, 128))
    tn = _tile(n, (1024, 512, 256, 128))
    return pl.pallas_call(
        _mm_kernel,
        out_shape=jax.ShapeDtypeStruct((m, n), out_dtype),
        grid=(m // tm, n // tn),
        in_specs=[pl.BlockSpec((tm, k), lambda i, j: (i, 0)),
                  pl.BlockSpec((None, k, tn), lambda i, j: (l, 0, j))],
        out_specs=pl.BlockSpec((tm, tn), lambda i, j: (i, j)),
        compiler_params=_cparams("parallel", "arbitrary"),
        name=name,
    )(x, w)


def _mm_res_kernel(x_ref, w_ref, r_ref, o_ref):
    o_ref[...] = r_ref[...] + _dot(x_ref[...], w_ref[...])


def _matmul_residual(x, w, l, res, name):
    m, k = x.shape
    n = w.shape[2]
    tm = _tile(m, tuple(t for t in (1024, 512, 256, 128) if t * k <= 4 * 1024 * 1024))
    tn = _tile(n, (512, 256, 128))
    return pl.pallas_call(
        _mm_res_kernel,
        out_shape=jax.ShapeDtypeStruct((m, n), F32),
        grid=(m // tm, n // tn),
        in_specs=[pl.BlockSpec((tm, k), lambda i, j: (i, 0)),
                  pl.BlockSpec((None, k, tn), lambda i, j: (l, 0, j)),
                  pl.BlockSpec((tm, tn), lambda i, j: (i, j))],
        out_specs=pl.BlockSpec((tm, tn), lambda i, j: (i, j)),
        compiler_params=_cparams("parallel", "arbitrary"),
        name=name,
    )(x, w, res)


def _gates_kernel(hn_ref, ws_ref, prm_ref, sel_ref, gexp_ref, bexp_ref, cexp_ref, rows_ref,
                  carry_ref, *, heads, tm):
    i = pl.program_id(0)

    @pl.when(i == 0)
    def _():
        carry_ref[...] = jnp.zeros_like(carry_ref)

    zs = _dot(hn_ref[...], ws_ref[...])
    lane = lax.broadcasted_iota(jnp.int32, zs.shape, 1)
    in_a = lane < heads
    in_b = (lane >= heads) & (lane < 2 * heads)
    in_f = (lane >= 2 * heads) & (lane < 3 * heads)
    a_log = prm_ref[0:1, :]
    dt_b = prm_ref[1:2, :]
    f_b = prm_ref[2:3, :]
    log_alpha = jnp.where(in_a, -jnp.exp(a_log) * _softplus(zs + dt_b), 0.0)
    beta = jnp.where(in_b, _sigmoid(zs), 0.0)
    log_f = jnp.where(in_f, -_softplus(-(zs + f_b)), 0.0)

    row = lax.broadcasted_iota(jnp.int32, (tm, tm), 0)
    col = lax.broadcasted_iota(jnp.int32, (tm, tm), 1)
    tri = col <= row
    tri_all = jnp.where(tri, 1.0, 0.0).astype(BF16)
    tri_chunk = jnp.where(tri & ((row // CHUNK) == (col // CHUNK)), 1.0, 0.0).astype(BF16)

    g_cum = _dot_exact_rhs(tri_chunk, log_alpha)
    c_cum = _dot_exact_rhs(tri_all, log_f) + jnp.where(in_f, carry_ref[...], 0.0)
    carry_ref[...] = c_cum[tm - 1:tm, :]

    packed = g_cum + beta + c_cum
    gw = gexp_ref.shape[1]
    gexp_ref[...] = _dot_exact_lhs(packed, sel_ref[:, 0:gw])
    bexp_ref[...] = _dot_exact_lhs(packed, sel_ref[:, gw:2 * gw])
    cexp_ref[...] = _dot_exact_lhs(packed, sel_ref[:, 2 * gw:3 * gw])
    rows_ref[...] = packed.T


def _gates(hn, w_small, l, prm, sel, heads):
    s, d = hn.shape
    gw = heads * HEAD_DIM
    tm = _tile(s, (256, 128, 64))
    exp_spec = pl.BlockSpec((tm, gw), lambda i: (i, 0))
    return pl.pallas_call(
        functools.partial(_gates_kernel, heads=heads, tm=tm),
        out_shape=(jax.ShapeDtypeStruct((s, gw), F32),) * 3 + (jax.ShapeDtypeStruct((LANES, s), F32),),
        grid=(s // tm,),
        in_specs=[pl.BlockSpec((tm, d), lambda i: (i, 0)),
                  pl.BlockSpec((None, d, LANES), lambda i: (l, 0, 0)),
                  pl.BlockSpec((8, LANES), lambda i: (0, 0)),
                  pl.BlockSpec((LANES, 3 * gw), lambda i: (0, 0))],
        out_specs=(exp_spec, exp_spec, exp_spec, pl.BlockSpec((LANES, tm), lambda i: (0, i))),
        scratch_shapes=[pltpu.VMEM((1, LANES), F32)],
        compiler_params=_cparams("arbitrary"),
        name="gates",
    )(hn, w_small, prm, sel)


def _gdn_prep_kernel(q_ref, k_ref, v_ref, qh_ref, kh_ref, vh_ref, cq_ref, ck_ref, cv_ref,
                     gexp_ref, bexp_ref, rows_ref,
                     kw_ref, b_ref, qeff_ref, o0_ref, gl_ref, *, tm):
    i = pl.program_id(0)
    h = pl.program_id(1)
    first = i == 0

    def conv_silu(x_ref, halo_ref, w_ref):
        x = x_ref[...].astype(F32)
        halo = jnp.where(first, 0.0, halo_ref[...].astype(F32))
        xe = jnp.concatenate([halo, x], axis=0)
        w = w_ref[...]
        y = w[SHORT_CONV - 1:SHORT_CONV, :] * x
        for kk in range(SHORT_CONV - 1):
            off = HALO - (SHORT_CONV - 1) + kk
            y = y + w[kk:kk + 1, :] * xe[off:off + tm, :]
        return _silu(y)

    q = conv_silu(q_ref, qh_ref, cq_ref)
    k = conv_silu(k_ref, kh_ref, ck_ref)
    v = conv_silu(v_ref, vh_ref, cv_ref)
    q = q * (lax.rsqrt(jnp.sum(q * q, axis=-1, keepdims=True) + NORM_EPS) * (HEAD_DIM ** -0.5))
    k = k * lax.rsqrt(jnp.sum(k * k, axis=-1, keepdims=True) + NORM_EPS)

    g_all = gexp_ref[...]
    bt_all = bexp_ref[...]
    g_rows = rows_ref[pl.ds(h % 8, 1), :]

    ii = lax.broadcasted_iota(jnp.int32, (CHUNK, CHUNK), 0)
    jj = lax.broadcasted_iota(jnp.int32, (CHUNK, CHUNK), 1)
    causal = jj <= ii
    strict = jj < ii

    sls = [slice(c * CHUNK, (c + 1) * CHUNK) for c in range(tm // CHUNK)]
    gs = [g_all[sl] for sl in sls]
    bts = [bt_all[sl] for sl in sls]
    egs = [jnp.exp(g) for g in gs]
    g_lasts = [g[CHUNK - 1:CHUNK, :] for g in gs]
    decays = [jnp.exp(jnp.where(causal, g[:, 0:CHUNK] - g_rows[:, sl], -jnp.inf)) for g, sl in zip(gs, sls)]
    qkks = [_dot_nt(jnp.concatenate([q[sl], k[sl]], axis=0).astype(BF16), k[sl].astype(BF16)) for sl in sls]
    qks = [(qkk[0:CHUNK] * d).astype(BF16) for qkk, d in zip(qkks, decays)]
    ps = [jnp.where(strict, -(bt[:, 0:CHUNK] * qkk[CHUNK:2 * CHUNK] * d), 0.0).astype(BF16)
          for bt, qkk, d in zip(bts, qkks, decays)]
    xs = [jnp.concatenate([(bt * eg) * k[sl], bt * v[sl]], axis=1) for bt, eg, sl in zip(bts, egs, sls)]
    levels = CHUNK.bit_length() - 1
    for lev in range(levels):
        xs = [x + _dot(p, x.astype(BF16)) for p, x in zip(ps, xs)]
        if lev + 1 < levels:
            ps = [_dot(p, p).astype(BF16) for p in ps]
    wubs = [x.astype(BF16) for x in xs]
    k_decs = [(k[sl] * jnp.exp(gl - g)).astype(BF16) for sl, gl, g in zip(sls, g_lasts, gs)]
    kwbs = [_dot_tn(kd, wub) for kd, wub in zip(k_decs, wubs)]
    qos = [_dot(qk, wub) for qk, wub in zip(qks, wubs)]
    for c, sl in enumerate(sls):
        kw_ref[c] = kwbs[c][:, 0:HEAD_DIM].astype(BF16)
        b_ref[c] = kwbs[c][:, HEAD_DIM:2 * HEAD_DIM]
        qeff_ref[sl, :] = (q[sl] * egs[c] - qos[c][:, 0:HEAD_DIM]).astype(BF16)
        o0_ref[sl, :] = qos[c][:, HEAD_DIM:2 * HEAD_DIM]
        gl_ref[c] = jnp.exp(g_lasts[c])


def _gdn_prep(z, conv_w, gexp, bexp, rows, heads):
    s = z.shape[0]
    gw = heads * HEAD_DIM
    nc = s // CHUNK
    tm = _tile(s, (512, 256, 128, 64))
    cpt = tm // CHUNK
    hb = tm // HALO

    def col(off):
        return lambda i, h: (i, off + h)

    def halo(off):
        return lambda i, h: (jnp.maximum(i * hb - 1, 0), off + h)

    def wcol(off):
        return lambda i, h: (0, off + h)

    blk = pl.BlockSpec((tm, HEAD_DIM), col(0))
    return pl.pallas_call(
        functools.partial(_gdn_prep_kernel, tm=tm),
        out_shape=(jax.ShapeDtypeStruct((nc, HEAD_DIM, gw), BF16),
                   jax.ShapeDtypeStruct((nc, HEAD_DIM, gw), F32),
                   jax.ShapeDtypeStruct((s, gw), BF16),
                   jax.ShapeDtypeStruct((s, gw), F32),
                   jax.ShapeDtypeStruct((nc, 1, gw), F32)),
        grid=(s // tm, heads),
        in_specs=[pl.BlockSpec((tm, HEAD_DIM), col(0)),
                  pl.BlockSpec((tm, HEAD_DIM), col(heads)),
                  pl.BlockSpec((tm, HEAD_DIM), col(2 * heads)),
                  pl.BlockSpec((HALO, HEAD_DIM), halo(0)),
                  pl.BlockSpec((HALO, HEAD_DIM), halo(heads)),
                  pl.BlockSpec((HALO, HEAD_DIM), halo(2 * heads)),
                  pl.BlockSpec((SHORT_CONV, HEAD_DIM), wcol(0)),
                  pl.BlockSpec((SHORT_CONV, HEAD_DIM), wcol(heads)),
                  pl.BlockSpec((SHORT_CONV, HEAD_DIM), wcol(2 * heads)),
                  blk, blk,
                  pl.BlockSpec((8, tm), lambda i, h: (h // 8, i))],
        out_specs=(pl.BlockSpec((cpt, HEAD_DIM, HEAD_DIM), lambda i, h: (i, 0, h)),
                   pl.BlockSpec((cpt, HEAD_DIM, HEAD_DIM), lambda i, h: (i, 0, h)),
                   blk, blk,
                   pl.BlockSpec((cpt, 1, HEAD_DIM), lambda i, h: (i, 0, h))),
        compiler_params=_cparams("parallel", "parallel"),
        name="gdn_prep",
    )(z, z, z, z, z, z, conv_w, conv_w, conv_w, gexp, bexp, rows)


def _gdn_scan_kernel(kw_ref, b_ref, qeff_ref, o0_ref, gl_ref, gate_ref, gn_ref, o_ref, s_ref,
                     *, cpt, group):
    @pl.when(pl.program_id(1) == 0)
    def _():
        s_ref[...] = jnp.zeros_like(s_ref)

    gn = gn_ref[...]
    for c in range(cpt):
        rows = slice(c * CHUNK, (c + 1) * CHUNK)
        for g in range(group):
            cols = slice(g * HEAD_DIM, (g + 1) * HEAD_DIM)
            st = s_ref[g]
            sb = st.astype(BF16)
            o = _dot(qeff_ref[rows, cols], sb) + o0_ref[rows, cols]
            s_ref[g] = gl_ref[c, :, cols] * st + (b_ref[c, :, cols] - _dot(kw_ref[c, :, cols], sb))
            r = lax.rsqrt(jnp.mean(o * o, axis=-1, keepdims=True) + NORM_EPS)
            o_ref[rows, cols] = ((o * r) * gn * _silu(gate_ref[rows, cols].astype(F32))).astype(o_ref.dtype)


def _gdn_scan(kw, b, qeff, o0, gl, z, gn, heads):
    s = qeff.shape[0]
    gw = heads * HEAD_DIM
    tm = _tile(s, (256, 128, 64))
    cpt = tm // CHUNK
    group = 8 if heads % 8 == 0 else heads
    gwid = group * HEAD_DIM
    gate_off = 3 * gw // gwid
    mat = pl.BlockSpec((cpt, HEAD_DIM, gwid), lambda hg, i: (i, 0, hg))
    blk = pl.BlockSpec((tm, gwid), lambda hg, i: (i, hg))
    return pl.pallas_call(
        functools.partial(_gdn_scan_kernel, cpt=cpt, group=group),
        out_shape=jax.ShapeDtypeStruct((s, gw), BF16),
        grid=(heads // group, s // tm),
        in_specs=[mat, mat, blk, blk,
                  pl.BlockSpec((cpt, 1, gwid), lambda hg, i: (i, 0, hg)),
                  pl.BlockSpec((tm, gwid), lambda hg, i: (i, gate_off + hg)),
                  pl.BlockSpec((1, HEAD_DIM), lambda hg, i: (0, 0))],
        out_specs=blk,
        scratch_shapes=[pltpu.VMEM((group, HEAD_DIM, HEAD_DIM), F32)],
        compiler_params=_cparams("parallel", "arbitrary"),
        name="gdn_scan",
    )(kw, b, qeff, o0, gl, z, gn.reshape(1, HEAD_DIM))


def _fox_prep_kernel(q_ref, k_ref, v_ref, c_ref, qa_ref, ka_ref, vt_ref):
    lane = lax.broadcasted_iota(jnp.int32, c_ref.shape, 1)
    qs = (q_ref[...].astype(F32) * (HEAD_DIM ** -0.5 * LOG2E)).astype(BF16)
    qa_ref[...] = jnp.concatenate([qs, jnp.where(lane < 3, 1.0, 0.0).astype(BF16)], axis=1)
    c1, c2, c3 = _split3(c_ref[...] * (-LOG2E))
    zero = jnp.zeros_like(c1)
    pieces = jnp.where(lane == 0, c1, jnp.where(lane == 1, c2, jnp.where(lane == 2, c3, zero)))
    ka_ref[...] = jnp.concatenate([k_ref[...], pieces], axis=1)
    vt_ref[0:HEAD_DIM, :] = v_ref[...].astype(F32).T.astype(BF16)
    vt_ref[HEAD_DIM:VT_ROWS, :] = jnp.ones((VT_ROWS - HEAD_DIM, vt_ref.shape[1]), BF16)


def _fox_prep(z, cexp, heads):
    s = z.shape[0]
    gw = heads * HEAD_DIM
    ts = _tile(s, (2048, 1024, 512, 256, 128))
    q_off, k_off, v_off = 4 * heads, 5 * heads, 6 * heads
    aug = pl.BlockSpec((ts, 2 * HEAD_DIM), lambda i, h: (i, h))
    return pl.pallas_call(
        _fox_prep_kernel,
        out_shape=(jax.ShapeDtypeStruct((s, 2 * gw), BF16),
                   jax.ShapeDtypeStruct((s, 2 * gw), BF16),
                   jax.ShapeDtypeStruct((heads * VT_ROWS, s), BF16)),
        grid=(s // ts, heads),
        in_specs=[pl.BlockSpec((ts, HEAD_DIM), lambda i, h: (i, q_off + h)),
                  pl.BlockSpec((ts, HEAD_DIM), lambda i, h: (i, k_off + h)),
                  pl.BlockSpec((ts, HEAD_DIM), lambda i, h: (i, v_off + h)),
                  pl.BlockSpec((ts, HEAD_DIM), lambda i, h: (i, h))],
        out_specs=(aug, aug, pl.BlockSpec((VT_ROWS, ts), lambda i, h: (h, i))),
        compiler_params=_cparams("parallel", "parallel"),
        name="fox_prep",
    )(z, z, z, cexp)


def _fox_kernel(qa_ref, ka_ref, vt_ref, o_ref, sa_ref, sb_ref, m_ref, acc_ref, *, tq):
    i = pl.program_id(1)
    tk = tq // 2
    qa = qa_ref[...]
    qw = min(tq, 2 * LANES)

    def scores(j):
        start = pl.multiple_of(j * tk, tk)
        return _dot_nt(ka_ref[pl.ds(start, tk), :], qa)

    def block(s_ref, j, diag_offset):
        vt = vt_ref[:, pl.ds(pl.multiple_of(j * tk, tk), tk)]
        for c0 in range(0, tq, qw):
            cols = slice(c0, c0 + qw)
            s = s_ref[:, cols]
            if diag_offset is not None:
                key = lax.broadcasted_iota(jnp.int32, (tk, qw), 0) + diag_offset
                qry = lax.broadcasted_iota(jnp.int32, (tk, qw), 1) + c0
                s = jnp.where(key <= qry, s, -jnp.inf)
            m_old = m_ref[:, cols]
            m_new = jnp.maximum(m_old, jnp.max(s, axis=0, keepdims=True))
            p = jnp.exp2(s - m_new).astype(BF16)
            acc_ref[:, cols] = jnp.exp2(m_old - m_new) * acc_ref[:, cols] + _dot(vt, p)
            m_ref[:, cols] = m_new

    m_ref[...] = jnp.full(m_ref.shape, -1e30, F32)
    acc_ref[...] = jnp.zeros_like(acc_ref)
    sa_ref[...] = scores(0)

    def pair(j):
        sb_ref[...] = scores(j + 1)
        block(sa_ref, j, None)
        sa_ref[...] = scores(j + 2)
        block(sb_ref, j + 1, None)

    def body(jj, carry):
        pair(4 * jj)
        pair(4 * jj + 2)
        return carry

    lax.fori_loop(0, i // 2, body, 0)

    @pl.when(i % 2 == 1)
    def _():
        pair(2 * i - 2)

    sb_ref[...] = scores(2 * i + 1)
    block(sa_ref, 2 * i, 0)
    block(sb_ref, 2 * i + 1, tk)

    o = acc_ref[0:HEAD_DIM, :] / acc_ref[HEAD_DIM:HEAD_DIM + 1, :]
    o_ref[...] = o.T.astype(o_ref.dtype)


def _fox(qa, ka, vt, heads):
    s = qa.shape[0]
    gw = heads * HEAD_DIM
    tq = _tile(s, (1024, 512, 256))
    return pl.pallas_call(
        functools.partial(_fox_kernel, tq=tq),
        out_shape=jax.ShapeDtypeStruct((s, gw), BF16),
        grid=(heads, s // tq),
        in_specs=[pl.BlockSpec((tq, 2 * HEAD_DIM), lambda h, i: (i, h)),
                  pl.BlockSpec((s, 2 * HEAD_DIM), lambda h, i: (0, h)),
                  pl.BlockSpec((VT_ROWS, s), lambda h, i: (h, 0))],
        out_specs=pl.BlockSpec((tq, HEAD_DIM), lambda h, i: (i, h)),
        scratch_shapes=[pltpu.VMEM((tq // 2, tq), F32),
                        pltpu.VMEM((tq // 2, tq), F32),
                        pltpu.VMEM((1, tq), F32),
                        pltpu.VMEM((VT_ROWS, tq), F32)],
        compiler_params=_cparams("parallel", "arbitrary"),
        name="fox_attention",
    )(qa, ka, vt)


def _merge_kernel(oa_ref, ob_ref, wa_ref, wb_ref, ga_ref, gb_ref, o_ref):
    ya = _dot(oa_ref[...], wa_ref[...])
    yb = _dot(ob_ref[...], wb_ref[...])
    y = _sigmoid(ga_ref[...].astype(F32)) * ya + _sigmoid(gb_ref[...].astype(F32)) * yb
    o_ref[...] = y.astype(o_ref.dtype)


def _merge(oa, ob, wa, wb, l, z, heads):
    s, gw = oa.shape
    d = wa.shape[2]
    tm = _tile(s, (1024, 512, 256, 128))
    tn = _tile(math.gcd(7 * gw, d), (512, 256, 128))
    ga_off = 7 * gw // tn
    gb_off = (7 * gw + d) // tn
    return pl.pallas_call(
        _merge_kernel,
        out_shape=jax.ShapeDtypeStruct((s, d), BF16),
        grid=(s // tm, d // tn),
        in_specs=[pl.BlockSpec((tm, gw), lambda i, j: (i, 0)),
                  pl.BlockSpec((tm, gw), lambda i, j: (i, 0)),
                  pl.BlockSpec((None, gw, tn), lambda i, j: (l, 0, j)),
                  pl.BlockSpec((None, gw, tn), lambda i, j: (l, 0, j)),
                  pl.BlockSpec((tm, tn), lambda i, j: (i, ga_off + j)),
                  pl.BlockSpec((tm, tn), lambda i, j: (i, gb_off + j))],
        out_specs=pl.BlockSpec((tm, tn), lambda i, j: (i, j)),
        compiler_params=_cparams("parallel", "arbitrary"),
        name="merge",
    )(oa, ob, wa, wb, z, z)


def _ffn_up_kernel(x_ref, wa_ref, wb_ref, ca_ref, cb_ref, o_ref, ta_ref, tb_ref, *, tm, cw):
    @pl.when(pl.program_id(1) == 0)
    def _():
        ta_ref[...] = jnp.zeros_like(ta_ref)
        tb_ref[...] = jnp.zeros_like(tb_ref)

    x = x_ref[...]

    def conv(u, tail_ref, c_ref, cols):
        ue = jnp.concatenate([tail_ref[:, cols], u], axis=0)
        c = c_ref[:, cols]
        y = c[FFN_CONV - 1:FFN_CONV, :] * u
        for kk in range(FFN_CONV - 1):
            off = TAIL - (FFN_CONV - 1) + kk
            y = y + c[kk:kk + 1, :] * ue[off:off + tm, :]
        tail_ref[:, cols] = u[tm - TAIL:tm, :]
        return y

    for c0 in range(0, o_ref.shape[1], cw):
        cols = slice(c0, c0 + cw)
        a = conv(_dot(x, wa_ref[:, cols]), ta_ref, ca_ref, cols)
        b = conv(_dot(x, wb_ref[:, cols]), tb_ref, cb_ref, cols)
        o_ref[:, cols] = (_silu(a) * b).astype(o_ref.dtype)


def _ffn_up(hn, w_up, l, conv_w):
    s, d = hn.shape
    dff = w_up.shape[2] // 2
    tm = _tile(s, (1024, 512, 256, 128, 64))
    tn = _tile(dff, (512, 256, 128))
    cw = _tile(tn, (256, 128))
    nb = dff // tn
    return pl.pallas_call(
        functools.partial(_ffn_up_kernel, tm=tm, cw=cw),
        out_shape=jax.ShapeDtypeStruct((s, dff), BF16),
        grid=(nb, s // tm),
        in_specs=[pl.BlockSpec((tm, d), lambda j, i: (i, 0)),
                  pl.BlockSpec((None, d, tn), lambda j, i: (l, 0, j)),
                  pl.BlockSpec((None, d, tn), lambda j, i: (l, 0, nb + j)),
                  pl.BlockSpec((FFN_CONV, tn), lambda j, i: (0, j)),
                  pl.BlockSpec((FFN_CONV, tn), lambda j, i: (0, nb + j))],
        out_specs=pl.BlockSpec((tm, tn), lambda j, i: (i, j)),
        scratch_shapes=[pltpu.VMEM((TAIL, tn), F32), pltpu.VMEM((TAIL, tn), F32)],
        compiler_params=_cparams("parallel", "arbitrary"),
        name="ffn_up",
    )(hn, w_up, w_up, conv_w, conv_w)


def _mixer_weights(w_in, heads):
    depth, d, _ = w_in.shape
    gw = heads * HEAD_DIM
    o_small = 4 * gw
    o_fqkv = o_small + 2 * heads
    o_ff = o_fqkv + 3 * gw
    o_gates = o_ff + heads
    w_main = jnp.concatenate(
        [w_in[:, :, 0:o_small], w_in[:, :, o_fqkv:o_ff], w_in[:, :, o_gates:o_gates + 2 * d]],
        axis=2).astype(BF16)
    w_small = jnp.concatenate(
        [w_in[:, :, o_small:o_fqkv], w_in[:, :, o_ff:o_gates],
         jnp.zeros((depth, d, LANES - 3 * heads), w_in.dtype)], axis=2).astype(BF16)
    return w_main, w_small


def _gate_params(a_log, dt_bias, f_bias, heads):
    prm = jnp.zeros((8, LANES), F32)
    prm = prm.at[0, 0:heads].set(a_log.astype(F32))
    prm = prm.at[1, 0:heads].set(dt_bias.astype(F32))
    prm = prm.at[2, 2 * heads:3 * heads].set(f_bias.astype(F32))
    return prm


def _selector(heads):
    gw = heads * HEAD_DIM
    src = lax.broadcasted_iota(jnp.int32, (LANES, 3 * gw), 0)
    dst = lax.broadcasted_iota(jnp.int32, (LANES, 3 * gw), 1)
    want = (dst // gw) * heads + (dst % gw) // HEAD_DIM
    return jnp.where(src == want, 1.0, 0.0).astype(BF16)


def kernel(x, mix_norm, w_in, gdn_conv, gdn_a_log, gdn_dt_bias, gdn_norm, fox_f_bias,
           w_proj_a, w_proj_b, w_out, ffn_norm, w_up, ffn_conv, w_down, final_norm):
    batch, seq, d = x.shape
    depth = w_in.shape[0]
    heads = d // (2 * HEAD_DIM)
    assert 3 * heads <= LANES and seq % CHUNK == 0
    sel = _selector(heads)
    w_main, w_small = _mixer_weights(w_in, heads)
    wa, wb, wo, wu, wd = (w.astype(BF16) for w in (w_proj_a, w_proj_b, w_out, w_up, w_down))
    outs = []
    for bi in range(batch):
        h = x[bi].astype(F32)
        for l in range(depth):
            hn = _rmsnorm(h, mix_norm[l], BF16)
            z = _matmul(hn, w_main, l, BF16, "in_proj")
            prm = _gate_params(gdn_a_log[l], gdn_dt_bias[l], fox_f_bias[l], heads)
            gexp, bexp, cexp, rows = _gates(hn, w_small, l, prm, sel, heads)
            kw, b, qeff, o0, gl = _gdn_prep(z, gdn_conv[l].astype(F32), gexp, bexp, rows, heads)
            oa = _gdn_scan(kw, b, qeff, o0, gl, z, gdn_norm[l].astype(F32), heads)
            qa, ka, vt = _fox_prep(z, cexp, heads)
            ob = _fox(qa, ka, vt, heads)
            y = _merge(oa, ob, wa, wb, l, z, heads)
            h = _matmul_residual(y, wo, l, h, "out_proj")
            hn = _rmsnorm(h, ffn_norm[l], BF16)
            act = _ffn_up(hn, wu, l, ffn_conv[l].astype(F32))
            h = _matmul_residual(act, wd, l, h, "ffn_down")
        outs.append(_rmsnorm(h, final_norm, x.dtype))
    return jnp.stack(outs, axis=0)
```

```python
import functools
import math

import jax
import jax.numpy as jnp
from jax import lax
from jax.experimental import pallas as pl
from jax.experimental.pallas import tpu as pltpu

F32 = jnp.float32
BF16 = jnp.bfloat16

HEAD_DIM = 128
CHUNK = 64
SHORT_CONV = 4
FFN_CONV = 3
NORM_EPS = 1e-6
LOG2E = 1.4426950408889634
LANES = 128
VT_ROWS = HEAD_DIM + 16
TAIL = 8
HALO = 16
VMEM_LIMIT = 56 * 1024 * 1024


def _cparams(*sem):
    return pltpu.CompilerParams(dimension_semantics=sem, vmem_limit_bytes=VMEM_LIMIT)


def _tile(n, prefs):
    for t in prefs:
        if n % t == 0:
            return t
    return n


def _dot(a, b):
    return jnp.dot(a, b, preferred_element_type=F32)


def _dot_nt(a, b):
    return lax.dot_general(a, b, (((1,), (1,)), ((), ())), preferred_element_type=F32)


def _dot_tn(a, b):
    return lax.dot_general(a, b, (((0,), (0,)), ((), ())), preferred_element_type=F32)


def _split3(x):
    x1 = x.astype(BF16)
    r1 = x - x1.astype(F32)
    x2 = r1.astype(BF16)
    r2 = r1 - x2.astype(F32)
    return x1, x2, r2.astype(BF16)


def _dot_exact_rhs(sel, x):
    x1, x2, x3 = _split3(x)
    return _dot(sel, x1) + _dot(sel, x2) + _dot(sel, x3)


def _dot_exact_lhs(x, sel):
    x1, x2, x3 = _split3(x)
    return _dot(x1, sel) + _dot(x2, sel) + _dot(x3, sel)


def _sigmoid(x):
    return 1.0 / (1.0 + jnp.exp(-x))


def _silu(x):
    return x * _sigmoid(x)


def _softplus(x):
    return jnp.maximum(x, 0.0) + jnp.log1p(jnp.exp(-jnp.abs(x)))


def _rmsnorm_kernel(x_ref, g_ref, o_ref):
    x = x_ref[...]
    r = lax.rsqrt(jnp.mean(x * x, axis=-1, keepdims=True) + NORM_EPS)
    o_ref[...] = ((x * r) * g_ref[...]).astype(o_ref.dtype)


def _rmsnorm(x, g, out_dtype):
    s, d = x.shape
    tm = _tile(s, (256, 128, 64, 32, 16))
    return pl.pallas_call(
        _rmsnorm_kernel,
        out_shape=jax.ShapeDtypeStruct((s, d), out_dtype),
        grid=(s // tm,),
        in_specs=[pl.BlockSpec((tm, d), lambda i: (i, 0)),
                  pl.BlockSpec((1, d), lambda i: (0, 0))],
        out_specs=pl.BlockSpec((tm, d), lambda i: (i, 0)),
        compiler_params=_cparams("parallel"),
        name="rmsnorm",
    )(x, g.reshape(1, d))


def _mm_kernel(x_ref, w_ref, o_ref):
    o_ref[...] = _dot(x_ref[...], w_ref[...]).astype(o_ref.dtype)


def _matmul(x, w, l, out_dtype, name):
    m, k = x.shape
    n = w.shape[2]
    tm = _tile(m, (1024, 512, 256, 128))
    tn = _tile(n, (1024, 512, 256, 128))
    return pl.pallas_call(
        _mm_kernel,
        out_shape=jax.ShapeDtypeStruct((m, n), out_dtype),
        grid=(m // tm, n // tn),
        in_specs=[pl.BlockSpec((tm, k), lambda i, j: (i, 0)),
                  pl.BlockSpec((None, k, tn), lambda i, j: (l, 0, j))],
        out_specs=pl.BlockSpec((tm, tn), lambda i, j: (i, j)),
        compiler_params=_cparams("parallel", "arbitrary"),
        name=name,
    )(x, w)


def _mm_res_kernel(x_ref, w_ref, r_ref, o_ref):
    o_ref[...] = r_ref[...] + _dot(x_ref[...], w_ref[...])


def _matmul_residual(x, w, l, res, name):
    m, k = x.shape
    n = w.shape[2]
    tm = _tile(m, tuple(t for t in (1024, 512, 256, 128) if t * k <= 4 * 1024 * 1024))
    tn = _tile(n, (512, 256, 128))
    return pl.pallas_call(
        _mm_res_kernel,
        out_shape=jax.ShapeDtypeStruct((m, n), F32),
        grid=(m // tm, n // tn),
        in_specs=[pl.BlockSpec((tm, k), lambda i, j: (i, 0)),
                  pl.BlockSpec((None, k, tn), lambda i, j: (l, 0, j)),
                  pl.BlockSpec((tm, tn), lambda i, j: (i, j))],
        out_specs=pl.BlockSpec((tm, tn), lambda i, j: (i, j)),
        compiler_params=_cparams("parallel", "arbitrary"),
        name=name,
    )(x, w, res)


def _gates_kernel(hn_ref, ws_ref, prm_ref, sel_ref, gexp_ref, bexp_ref, cexp_ref, rows_ref,
                  carry_ref, *, heads, tm):
    i = pl.program_id(0)

    @pl.when(i == 0)
    def _():
        carry_ref[...] = jnp.zeros_like(carry_ref)

    zs = _dot(hn_ref[...], ws_ref[...])
    lane = lax.broadcasted_iota(jnp.int32, zs.shape, 1)
    in_a = lane < heads
    in_b = (lane >= heads) & (lane < 2 * heads)
    in_f = (lane >= 2 * heads) & (lane < 3 * heads)
    a_log = prm_ref[0:1, :]
    dt_b = prm_ref[1:2, :]
    f_b = prm_ref[2:3, :]
    log_alpha = jnp.where(in_a, -jnp.exp(a_log) * _softplus(zs + dt_b), 0.0)
    beta = jnp.where(in_b, _sigmoid(zs), 0.0)
    log_f = jnp.where(in_f, -_softplus(-(zs + f_b)), 0.0)

    row = lax.broadcasted_iota(jnp.int32, (tm, tm), 0)
    col = lax.broadcasted_iota(jnp.int32, (tm, tm), 1)
    tri = col <= row
    tri_all = jnp.where(tri, 1.0, 0.0).astype(BF16)
    tri_chunk = jnp.where(tri & ((row // CHUNK) == (col // CHUNK)), 1.0, 0.0).astype(BF16)

    g_cum = _dot_exact_rhs(tri_chunk, log_alpha)
    c_cum = _dot_exact_rhs(tri_all, log_f) + jnp.where(in_f, carry_ref[...], 0.0)
    carry_ref[...] = c_cum[tm - 1:tm, :]

    packed = g_cum + beta + c_cum
    gw = gexp_ref.shape[1]
    gexp_ref[...] = _dot_exact_lhs(packed, sel_ref[:, 0:gw])
    bexp_ref[...] = _dot_exact_lhs(packed, sel_ref[:, gw:2 * gw])
    cexp_ref[...] = _dot_exact_lhs(packed, sel_ref[:, 2 * gw:3 * gw])
    rows_ref[...] = packed.T


def _gates(hn, w_small, l, prm, sel, heads):
    s, d = hn.shape
    gw = heads * HEAD_DIM
    tm = _tile(s, (256, 128, 64))
    exp_spec = pl.BlockSpec((tm, gw), lambda i: (i, 0))
    return pl.pallas_call(
        functools.partial(_gates_kernel, heads=heads, tm=tm),
        out_shape=(jax.ShapeDtypeStruct((s, gw), F32),) * 3 + (jax.ShapeDtypeStruct((LANES, s), F32),),
        grid=(s // tm,),
        in_specs=[pl.BlockSpec((tm, d), lambda i: (i, 0)),
                  pl.BlockSpec((None, d, LANES), lambda i: (l, 0, 0)),
                  pl.BlockSpec((8, LANES), lambda i: (0, 0)),
                  pl.BlockSpec((LANES, 3 * gw), lambda i: (0, 0))],
        out_specs=(exp_spec, exp_spec, exp_spec, pl.BlockSpec((LANES, tm), lambda i: (0, i))),
        scratch_shapes=[pltpu.VMEM((1, LANES), F32)],
        compiler_params=_cparams("arbitrary"),
        name="gates",
    )(hn, w_small, prm, sel)


def _gdn_prep_kernel(q_ref, k_ref, v_ref, qh_ref, kh_ref, vh_ref, cq_ref, ck_ref, cv_ref,
                     gexp_ref, bexp_ref, rows_ref,
                     kw_ref, b_ref, qeff_ref, o0_ref, gl_ref, *, tm):
    i = pl.program_id(0)
    h = pl.program_id(1)
    first = i == 0

    def conv_silu(x_ref, halo_ref, w_ref):
        x = x_ref[...].astype(F32)
        halo = jnp.where(first, 0.0, halo_ref[...].astype(F32))
        xe = jnp.concatenate([halo, x], axis=0)
        w = w_ref[...]
        y = w[SHORT_CONV - 1:SHORT_CONV, :] * x
        for kk in range(SHORT_CONV - 1):
            off = HALO - (SHORT_CONV - 1) + kk
            y = y + w[kk:kk + 1, :] * xe[off:off + tm, :]
        return _silu(y)

    q = conv_silu(q_ref, qh_ref, cq_ref)
    k = conv_silu(k_ref, kh_ref, ck_ref)
    v = conv_silu(v_ref, vh_ref, cv_ref)
    q = q * (lax.rsqrt(jnp.sum(q * q, axis=-1, keepdims=True) + NORM_EPS) * (HEAD_DIM ** -0.5))
    k = k * lax.rsqrt(jnp.sum(k * k, axis=-1, keepdims=True) + NORM_EPS)

    g_all = gexp_ref[...]
    bt_all = bexp_ref[...]
    g_rows = rows_ref[pl.ds(h % 8, 1), :]

    ii = lax.broadcasted_iota(jnp.int32, (CHUNK, CHUNK), 0)
    jj = lax.broadcasted_iota(jnp.int32, (CHUNK, CHUNK), 1)
    causal = jj <= ii
    strict = jj < ii

    sls = [slice(c * CHUNK, (c + 1) * CHUNK) for c in range(tm // CHUNK)]
    gs = [g_all[sl] for sl in sls]
    bts = [bt_all[sl] for sl in sls]
    egs = [jnp.exp(g) for g in gs]
    g_lasts = [g[CHUNK - 1:CHUNK, :] for g in gs]
    decays = [jnp.exp(jnp.where(causal, g[:, 0:CHUNK] - g_rows[:, sl], -jnp.inf)) for g, sl in zip(gs, sls)]
    qkks = [_dot_nt(jnp.concatenate([q[sl], k[sl]], axis=0).astype(BF16), k[sl].astype(BF16)) for sl in sls]
    qks = [(qkk[0:CHUNK] * d).astype(BF16) for qkk, d in zip(qkks, decays)]
    ps = [jnp.where(strict, -(bt[:, 0:CHUNK] * qkk[CHUNK:2 * CHUNK] * d), 0.0).astype(BF16)
          for bt, qkk, d in zip(bts, qkks, decays)]
    xs = [jnp.concatenate([(bt * eg) * k[sl], bt * v[sl]], axis=1) for bt, eg, sl in zip(bts, egs, sls)]
    levels = CHUNK.bit_length() - 1
    for lev in range(levels):
        xs = [x + _dot(p, x.astype(BF16)) for p, x in zip(ps, xs)]
        if lev + 1 < levels:
            ps = [_dot(p, p).astype(BF16) for p in ps]
    wubs = [x.astype(BF16) for x in xs]
    k_decs = [(k[sl] * jnp.exp(gl - g)).astype(BF16) for sl, gl, g in zip(sls, g_lasts, gs)]
    kwbs = [_dot_tn(kd, wub) for kd, wub in zip(k_decs, wubs)]
    qos = [_dot(qk, wub) for qk, wub in zip(qks, wubs)]
    for c, sl in enumerate(sls):
        kw_ref[c] = kwbs[c][:, 0:HEAD_DIM].astype(BF16)
        b_ref[c] = kwbs[c][:, HEAD_DIM:2 * HEAD_DIM].astype(BF16)
        qeff_ref[sl, :] = (q[sl] * egs[c] - qos[c][:, 0:HEAD_DIM]).astype(BF16)
        o0_ref[sl, :] = qos[c][:, HEAD_DIM:2 * HEAD_DIM].astype(BF16)
        gl_ref[c] = jnp.exp(g_lasts[c])


def _gdn_prep(z, conv_w, gexp, bexp, rows, heads):
    s = z.shape[0]
    gw = heads * HEAD_DIM
    nc = s // CHUNK
    tm = _tile(s, (512, 256, 128, 64))
    cpt = tm // CHUNK
    hb = tm // HALO

    def col(off):
        return lambda i, h: (i, off + h)

    def halo(off):
        return lambda i, h: (jnp.maximum(i * hb - 1, 0), off + h)

    def wcol(off):
        return lambda i, h: (0, off + h)

    blk = pl.BlockSpec((tm, HEAD_DIM), col(0))
    return pl.pallas_call(
        functools.partial(_gdn_prep_kernel, tm=tm),
        out_shape=(jax.ShapeDtypeStruct((nc, HEAD_DIM, gw), BF16),
                   jax.ShapeDtypeStruct((nc, HEAD_DIM, gw), BF16),
                   jax.ShapeDtypeStruct((s, gw), BF16),
                   jax.ShapeDtypeStruct((s, gw), BF16),
                   jax.ShapeDtypeStruct((nc, 1, gw), F32)),
        grid=(s // tm, heads),
        in_specs=[pl.BlockSpec((tm, HEAD_DIM), col(0)),
                  pl.BlockSpec((tm, HEAD_DIM), col(heads)),
                  pl.BlockSpec((tm, HEAD_DIM), col(2 * heads)),
                  pl.BlockSpec((HALO, HEAD_DIM), halo(0)),
                  pl.BlockSpec((HALO, HEAD_DIM), halo(heads)),
                  pl.BlockSpec((HALO, HEAD_DIM), halo(2 * heads)),
                  pl.BlockSpec((SHORT_CONV, HEAD_DIM), wcol(0)),
                  pl.BlockSpec((SHORT_CONV, HEAD_DIM), wcol(heads)),
                  pl.BlockSpec((SHORT_CONV, HEAD_DIM), wcol(2 * heads)),
                  blk, blk,
                  pl.BlockSpec((8, tm), lambda i, h: (h // 8, i))],
        out_specs=(pl.BlockSpec((cpt, HEAD_DIM, HEAD_DIM), lambda i, h: (i, 0, h)),
                   pl.BlockSpec((cpt, HEAD_DIM, HEAD_DIM), lambda i, h: (i, 0, h)),
                   blk, blk,
                   pl.BlockSpec((cpt, 1, HEAD_DIM), lambda i, h: (i, 0, h))),
        compiler_params=_cparams("parallel", "parallel"),
        name="gdn_prep",
    )(z, z, z, z, z, z, conv_w, conv_w, conv_w, gexp, bexp, rows)


def _gdn_scan_kernel(kw_ref, b_ref, qeff_ref, o0_ref, gl_ref, gate_ref, gn_ref, o_ref, s_ref,
                     *, cpt, group):
    @pl.when(pl.program_id(1) == 0)
    def _():
        s_ref[...] = jnp.zeros_like(s_ref)

    gn = gn_ref[...]
    for c in range(cpt):
        rows = slice(c * CHUNK, (c + 1) * CHUNK)
        for g in range(group):
            cols = slice(g * HEAD_DIM, (g + 1) * HEAD_DIM)
            st = s_ref[g]
            sb = st.astype(BF16)
            o = _dot(qeff_ref[rows, cols], sb) + o0_ref[rows, cols].astype(F32)
            s_ref[g] = gl_ref[c, :, cols] * st + (b_ref[c, :, cols].astype(F32) - _dot(kw_ref[c, :, cols], sb))
            r = lax.rsqrt(jnp.mean(o * o, axis=-1, keepdims=True) + NORM_EPS)
            o_ref[rows, cols] = ((o * r) * gn * _silu(gate_ref[rows, cols].astype(F32))).astype(o_ref.dtype)


def _gdn_scan(kw, b, qeff, o0, gl, z, gn, heads):
    s = qeff.shape[0]
    gw = heads * HEAD_DIM
    tm = _tile(s, (256, 128, 64))
    cpt = tm // CHUNK
    group = 8 if heads % 8 == 0 else heads
    gwid = group * HEAD_DIM
    gate_off = 3 * gw // gwid
    mat = pl.BlockSpec((cpt, HEAD_DIM, gwid), lambda hg, i: (i, 0, hg))
    blk = pl.BlockSpec((tm, gwid), lambda hg, i: (i, hg))
    return pl.pallas_call(
        functools.partial(_gdn_scan_kernel, cpt=cpt, group=group),
        out_shape=jax.ShapeDtypeStruct((s, gw), BF16),
        grid=(heads // group, s // tm),
        in_specs=[mat, mat, blk, blk,
                  pl.BlockSpec((cpt, 1, gwid), lambda hg, i: (i, 0, hg)),
                  pl.BlockSpec((tm, gwid), lambda hg, i: (i, gate_off + hg)),
                  pl.BlockSpec((1, HEAD_DIM), lambda hg, i: (0, 0))],
        out_specs=blk,
        scratch_shapes=[pltpu.VMEM((group, HEAD_DIM, HEAD_DIM), F32)],
        compiler_params=_cparams("parallel", "arbitrary"),
        name="gdn_scan",
    )(kw, b, qeff, o0, gl, z, gn.reshape(1, HEAD_DIM))


def _fox_prep_kernel(q_ref, k_ref, v_ref, c_ref, qa_ref, ka_ref, vt_ref):
    lane = lax.broadcasted_iota(jnp.int32, c_ref.shape, 1)
    qs = (q_ref[...].astype(F32) * (HEAD_DIM ** -0.5 * LOG2E)).astype(BF16)
    qa_ref[...] = jnp.concatenate([qs, jnp.where(lane < 3, 1.0, 0.0).astype(BF16)], axis=1)
    c1, c2, c3 = _split3(c_ref[...] * (-LOG2E))
    zero = jnp.zeros_like(c1)
    pieces = jnp.where(lane == 0, c1, jnp.where(lane == 1, c2, jnp.where(lane == 2, c3, zero)))
    ka_ref[...] = jnp.concatenate([k_ref[...], pieces], axis=1)
    vt_ref[0:HEAD_DIM, :] = v_ref[...].astype(F32).T.astype(BF16)
    vt_ref[HEAD_DIM:VT_ROWS, :] = jnp.ones((VT_ROWS - HEAD_DIM, vt_ref.shape[1]), BF16)


def _fox_prep(z, cexp, heads):
    s = z.shape[0]
    gw = heads * HEAD_DIM
    ts = _tile(s, (2048, 1024, 512, 256, 128))
    q_off, k_off, v_off = 4 * heads, 5 * heads, 6 * heads
    aug = pl.BlockSpec((ts, 2 * HEAD_DIM), lambda i, h: (i, h))
    return pl.pallas_call(
        _fox_prep_kernel,
        out_shape=(jax.ShapeDtypeStruct((s, 2 * gw), BF16),
                   jax.ShapeDtypeStruct((s, 2 * gw), BF16),
                   jax.ShapeDtypeStruct((heads * VT_ROWS, s), BF16)),
        grid=(s // ts, heads),
        in_specs=[pl.BlockSpec((ts, HEAD_DIM), lambda i, h: (i, q_off + h)),
                  pl.BlockSpec((ts, HEAD_DIM), lambda i, h: (i, k_off + h)),
                  pl.BlockSpec((ts, HEAD_DIM), lambda i, h: (i, v_off + h)),
                  pl.BlockSpec((ts, HEAD_DIM), lambda i, h: (i, h))],
        out_specs=(aug, aug, pl.BlockSpec((VT_ROWS, ts), lambda i, h: (h, i))),
        compiler_params=_cparams("parallel", "parallel"),
        name="fox_prep",
    )(z, z, z, cexp)


def _fox_kernel(qa_ref, ka_ref, vt_ref, o_ref, sa_ref, sb_ref, m_ref, acc_ref, *, tq):
    i = pl.program_id(1)
    tk = tq // 2
    qa = qa_ref[...]
    qw = min(tk, 2 * LANES)

    def scores(j):
        start = pl.multiple_of(j * tk, tk)
        return _dot_nt(ka_ref[pl.ds(start, tk), :], qa)

    def block(s_ref, j, diag_offset):
        vt = vt_ref[:, pl.ds(pl.multiple_of(j * tk, tk), tk)]
        for c0 in range(diag_offset or 0, tq, qw):
            cols = slice(c0, c0 + qw)
            s = s_ref[:, cols]
            if diag_offset is not None:
                key = lax.broadcasted_iota(jnp.int32, (tk, qw), 0) + diag_offset
                qry = lax.broadcasted_iota(jnp.int32, (tk, qw), 1) + c0
                s = jnp.where(key <= qry, s, -jnp.inf)
            m_old = m_ref[:, cols]
            m_new = jnp.maximum(m_old, jnp.max(s, axis=0, keepdims=True))
            p = jnp.exp2(s - m_new).astype(BF16)
            acc_ref[:, cols] = jnp.exp2(m_old - m_new) * acc_ref[:, cols] + _dot(vt, p)
            m_ref[:, cols] = m_new

    m_ref[...] = jnp.full(m_ref.shape, -1e30, F32)
    acc_ref[...] = jnp.zeros_like(acc_ref)
    sa_ref[...] = scores(0)

    def pair(j):
        sb_ref[...] = scores(j + 1)
        block(sa_ref, j, None)
        sa_ref[...] = scores(j + 2)
        block(sb_ref, j + 1, None)

    def body(jj, carry):
        for u in range(4):
            pair(8 * jj + 2 * u)
        return carry

    lax.fori_loop(0, i // 4, body, 0)
    done = i - i % 4

    @pl.when(i % 4 >= 2)
    def _():
        pair(2 * done)
        pair(2 * done + 2)

    @pl.when(i % 2 == 1)
    def _():
        pair(2 * i - 2)

    last = pl.multiple_of((2 * i + 1) * tk, tk)
    sb_ref[:, tk:tq] = _dot_nt(ka_ref[pl.ds(last, tk), :], qa[tk:tq, :])
    block(sa_ref, 2 * i, 0)
    block(sb_ref, 2 * i + 1, tk)

    o = acc_ref[0:HEAD_DIM, :] / acc_ref[HEAD_DIM:HEAD_DIM + 1, :]
    o_ref[...] = o.T.astype(o_ref.dtype)


def _fox(qa, ka, vt, heads):
    s = qa.shape[0]
    gw = heads * HEAD_DIM
    tq = _tile(s, (1024, 512, 256))
    return pl.pallas_call(
        functools.partial(_fox_kernel, tq=tq),
        out_shape=jax.ShapeDtypeStruct((s, gw), BF16),
        grid=(heads, s // tq),
        in_specs=[pl.BlockSpec((tq, 2 * HEAD_DIM), lambda h, i: (i, h)),
                  pl.BlockSpec((s, 2 * HEAD_DIM), lambda h, i: (0, h)),
                  pl.BlockSpec((VT_ROWS, s), lambda h, i: (h, 0))],
        out_specs=pl.BlockSpec((tq, HEAD_DIM), lambda h, i: (i, h)),
        scratch_shapes=[pltpu.VMEM((tq // 2, tq), F32),
                        pltpu.VMEM((tq // 2, tq), F32),
                        pltpu.VMEM((1, tq), F32),
                        pltpu.VMEM((VT_ROWS, tq), F32)],
        compiler_params=_cparams("parallel", "arbitrary"),
        name="fox_attention",
    )(qa, ka, vt)


def _merge_kernel(oa_ref, ob_ref, wa_ref, wb_ref, ga_ref, gb_ref, o_ref):
    ya = _dot(oa_ref[...], wa_ref[...])
    yb = _dot(ob_ref[...], wb_ref[...])
    y = _sigmoid(ga_ref[...].astype(F32)) * ya + _sigmoid(gb_ref[...].astype(F32)) * yb
    o_ref[...] = y.astype(o_ref.dtype)


def _merge(oa, ob, wa, wb, l, z, heads):
    s, gw = oa.shape
    d = wa.shape[2]
    tm = _tile(s, (1024, 512, 256, 128))
    tn = _tile(math.gcd(7 * gw, d), (512, 256, 128))
    ga_off = 7 * gw // tn
    gb_off = (7 * gw + d) // tn
    return pl.pallas_call(
        _merge_kernel,
        out_shape=jax.ShapeDtypeStruct((s, d), BF16),
        grid=(s // tm, d // tn),
        in_specs=[pl.BlockSpec((tm, gw), lambda i, j: (i, 0)),
                  pl.BlockSpec((tm, gw), lambda i, j: (i, 0)),
                  pl.BlockSpec((None, gw, tn), lambda i, j: (l, 0, j)),
                  pl.BlockSpec((None, gw, tn), lambda i, j: (l, 0, j)),
                  pl.BlockSpec((tm, tn), lambda i, j: (i, ga_off + j)),
                  pl.BlockSpec((tm, tn), lambda i, j: (i, gb_off + j))],
        out_specs=pl.BlockSpec((tm, tn), lambda i, j: (i, j)),
        compiler_params=_cparams("parallel", "arbitrary"),
        name="merge",
    )(oa, ob, wa, wb, z, z)


def _ffn_up_kernel(x_ref, wa_ref, wb_ref, ca_ref, cb_ref, o_ref, ta_ref, tb_ref, *, tm, cw):
    @pl.when(pl.program_id(1) == 0)
    def _():
        ta_ref[...] = jnp.zeros_like(ta_ref)
        tb_ref[...] = jnp.zeros_like(tb_ref)

    x = x_ref[...]

    def conv(u, tail_ref, c_ref, cols):
        ue = jnp.concatenate([tail_ref[:, cols], u], axis=0)
        c = c_ref[:, cols]
        y = c[FFN_CONV - 1:FFN_CONV, :] * u
        for kk in range(FFN_CONV - 1):
            off = TAIL - (FFN_CONV - 1) + kk
            y = y + c[kk:kk + 1, :] * ue[off:off + tm, :]
        tail_ref[:, cols] = u[tm - TAIL:tm, :]
        return y

    for c0 in range(0, o_ref.shape[1], cw):
        cols = slice(c0, c0 + cw)
        a = conv(_dot(x, wa_ref[:, cols]), ta_ref, ca_ref, cols)
        b = conv(_dot(x, wb_ref[:, cols]), tb_ref, cb_ref, cols)
        o_ref[:, cols] = (_silu(a) * b).astype(o_ref.dtype)


def _ffn_up(hn, w_up, l, conv_w):
    s, d = hn.shape
    dff = w_up.shape[2] // 2
    tm = _tile(s, (1024, 512, 256, 128, 64))
    tn = _tile(dff, (512, 256, 128))
    cw = _tile(tn, (256, 128))
    nb = dff // tn
    return pl.pallas_call(
        functools.partial(_ffn_up_kernel, tm=tm, cw=cw),
        out_shape=jax.ShapeDtypeStruct((s, dff), BF16),
        grid=(nb, s // tm),
        in_specs=[pl.BlockSpec((tm, d), lambda j, i: (i, 0)),
                  pl.BlockSpec((None, d, tn), lambda j, i: (l, 0, j)),
                  pl.BlockSpec((None, d, tn), lambda j, i: (l, 0, nb + j)),
                  pl.BlockSpec((FFN_CONV, tn), lambda j, i: (0, j)),
                  pl.BlockSpec((FFN_CONV, tn), lambda j, i: (0, nb + j))],
        out_specs=pl.BlockSpec((tm, tn), lambda j, i: (i, j)),
        scratch_shapes=[pltpu.VMEM((TAIL, tn), F32), pltpu.VMEM((TAIL, tn), F32)],
        compiler_params=_cparams("parallel", "arbitrary"),
        name="ffn_up",
    )(hn, w_up, w_up, conv_w, conv_w)


def _mixer_weights(w_in, heads):
    depth, d, _ = w_in.shape
    gw = heads * HEAD_DIM
    o_small = 4 * gw
    o_fqkv = o_small + 2 * heads
    o_ff = o_fqkv + 3 * gw
    o_gates = o_ff + heads
    w_main = jnp.concatenate(
        [w_in[:, :, 0:o_small], w_in[:, :, o_fqkv:o_ff], w_in[:, :, o_gates:o_gates + 2 * d]],
        axis=2).astype(BF16)
    w_small = jnp.concatenate(
        [w_in[:, :, o_small:o_fqkv], w_in[:, :, o_ff:o_gates],
         jnp.zeros((depth, d, LANES - 3 * heads), w_in.dtype)], axis=2).astype(BF16)
    return w_main, w_small


def _gate_params(a_log, dt_bias, f_bias, heads):
    prm = jnp.zeros((8, LANES), F32)
    prm = prm.at[0, 0:heads].set(a_log.astype(F32))
    prm = prm.at[1, 0:heads].set(dt_bias.astype(F32))
    prm = prm.at[2, 2 * heads:3 * heads].set(f_bias.astype(F32))
    return prm


def _selector(heads):
    gw = heads * HEAD_DIM
    src = lax.broadcasted_iota(jnp.int32, (LANES, 3 * gw), 0)
    dst = lax.broadcasted_iota(jnp.int32, (LANES, 3 * gw), 1)
    want = (dst // gw) * heads + (dst % gw) // HEAD_DIM
    return jnp.where(src == want, 1.0, 0.0).astype(BF16)


def kernel(x, mix_norm, w_in, gdn_conv, gdn_a_log, gdn_dt_bias, gdn_norm, fox_f_bias,
           w_proj_a, w_proj_b, w_out, ffn_norm, w_up, ffn_conv, w_down, final_norm):
    batch, seq, d = x.shape
    depth = w_in.shape[0]
    heads = d // (2 * HEAD_DIM)
    assert 3 * heads <= LANES and seq % CHUNK == 0
    sel = _selector(heads)
    w_main, w_small = _mixer_weights(w_in, heads)
    wa, wb, wo, wu, wd = (w.astype(BF16) for w in (w_proj_a, w_proj_b, w_out, w_up, w_down))
    outs = []
    for bi in range(batch):
        h = x[bi].astype(F32)
        for l in range(depth):
            hn = _rmsnorm(h, mix_norm[l], BF16)
            z = _matmul(hn, w_main, l, BF16, "in_proj")
            prm = _gate_params(gdn_a_log[l], gdn_dt_bias[l], fox_f_bias[l], heads)
            gexp, bexp, cexp, rows = _gates(hn, w_small, l, prm, sel, heads)
            kw, b, qeff, o0, gl = _gdn_prep(z, gdn_conv[l].astype(F32), gexp, bexp, rows, heads)
            oa = _gdn_scan(kw, b, qeff, o0, gl, z, gdn_norm[l].astype(F32), heads)
            qa, ka, vt = _fox_prep(z, cexp, heads)
            ob = _fox(qa, ka, vt, heads)
            y = _merge(oa, ob, wa, wb, l, z, heads)
            h = _matmul_residual(y, wo, l, h, "out_proj")
            hn = _rmsnorm(h, ffn_norm[l], BF16)
            act = _ffn_up(hn, wu, l, ffn_conv[l].astype(F32))
            h = _matmul_residual(act, wd, l, h, "ffn_down")
        outs.append(_rmsnorm(h, final_norm, x.dtype))
    return jnp.stack(outs, axis=0)
```

```python
import functools
import math

import jax
import jax.numpy as jnp
from jax import lax
from jax.experimental import pallas as pl
from jax.experimental.pallas import tpu as pltpu

F32 = jnp.float32
BF16 = jnp.bfloat16

HEAD_DIM = 128
CHUNK = 64
SHORT_CONV = 4
FFN_CONV = 3
NORM_EPS = 1e-6
LOG2E = 1.4426950408889634
LANES = 128
VT_ROWS = HEAD_DIM + 16
TAIL = 8
HALO = 16
VMEM_LIMIT = 56 * 1024 * 1024


def _cparams(*sem):
    return pltpu.CompilerParams(dimension_semantics=sem, vmem_limit_bytes=VMEM_LIMIT)


def _tile(n, prefs):
    for t in prefs:
        if n % t == 0:
            return t
    return n


def _dot(a, b):
    return jnp.dot(a, b, preferred_element_type=F32)


def _dot_nt(a, b):
    return lax.dot_general(a, b, (((1,), (1,)), ((), ())), preferred_element_type=F32)


def _dot_tn(a, b):
    return lax.dot_general(a, b, (((0,), (0,)), ((), ())), preferred_element_type=F32)


def _split3(x):
    x1 = x.astype(BF16)
    r1 = x - x1.astype(F32)
    x2 = r1.astype(BF16)
    r2 = r1 - x2.astype(F32)
    return x1, x2, r2.astype(BF16)


def _dot_exact_rhs(sel, x):
    x1, x2, x3 = _split3(x)
    return _dot(sel, x1) + _dot(sel, x2) + _dot(sel, x3)


def _dot_exact_lhs(x, sel):
    x1, x2, x3 = _split3(x)
    return _dot(x1, sel) + _dot(x2, sel) + _dot(x3, sel)


def _sigmoid(x):
    return 1.0 / (1.0 + jnp.exp(-x))


def _silu(x):
    return x * _sigmoid(x)


def _softplus(x):
    return jnp.maximum(x, 0.0) + jnp.log1p(jnp.exp(-jnp.abs(x)))


def _rmsnorm_kernel(x_ref, g_ref, o_ref):
    x = x_ref[...]
    r = lax.rsqrt(jnp.mean(x * x, axis=-1, keepdims=True) + NORM_EPS)
    o_ref[...] = ((x * r) * g_ref[...]).astype(o_ref.dtype)


def _rmsnorm(x, g, out_dtype):
    s, d = x.shape
    tm = _tile(s, (256, 128, 64, 32, 16))
    return pl.pallas_call(
        _rmsnorm_kernel,
        out_shape=jax.ShapeDtypeStruct((s, d), out_dtype),
        grid=(s // tm,),
        in_specs=[pl.BlockSpec((tm, d), lambda i: (i, 0)),
                  pl.BlockSpec((1, d), lambda i: (0, 0))],
        out_specs=pl.BlockSpec((tm, d), lambda i: (i, 0)),
        compiler_params=_cparams("parallel"),
        name="rmsnorm",
    )(x, g.reshape(1, d))


def _mm_kernel(x_ref, w_ref, o_ref):
    o_ref[...] = _dot(x_ref[...], w_ref[...]).astype(o_ref.dtype)


def _matmul(x, w, l, out_dtype, name):
    m, k = x.shape
    n = w.shape[2]
    tm = _tile(m, (1024, 512, 256, 128))
    tn = _tile(n, (1024, 512, 256, 128))
    return pl.pallas_call(
        _mm_kernel,
        out_shape=jax.ShapeDtypeStruct((m, n), out_dtype),
        grid=(m // tm, n // tn),
        in_specs=[pl.BlockSpec((tm, k), lambda i, j: (i, 0)),
                  pl.BlockSpec((None, k, tn), lambda i, j: (l, 0, j))],
        out_specs=pl.BlockSpec((tm, tn), lambda i, j: (i, j)),
        compiler_params=_cparams("parallel", "arbitrary"),
        name=name,
    )(x, w)


def _mm_res_kernel(x_ref, w_ref, r_ref, o_ref):
    o_ref[...] = r_ref[...] + _dot(x_ref[...], w_ref[...])


def _matmul_residual(x, w, l, res, name):
    m, k = x.shape
    n = w.shape[2]
    tm = _tile(m, tuple(t for t in (1024, 512, 256, 128) if t * k <= 4 * 1024 * 1024))
    tn = _tile(n, (512, 256, 128))
    return pl.pallas_call(
        _mm_res_kernel,
        out_shape=jax.ShapeDtypeStruct((m, n), F32),
        grid=(m // tm, n // tn),
        in_specs=[pl.BlockSpec((tm, k), lambda i, j: (i, 0)),
                  pl.BlockSpec((None, k, tn), lambda i, j: (l, 0, j)),
                  pl.BlockSpec((tm, tn), lambda i, j: (i, j))],
        out_specs=pl.BlockSpec((tm, tn), lambda i, j: (i, j)),
        compiler_params=_cparams("parallel", "arbitrary"),
        name=name,
    )(x, w, res)


def _gates_kernel(hn_ref, ws_ref, prm_ref, sel_ref, gexp_ref, bexp_ref, cexp_ref, rows_ref,
                  carry_ref, *, heads, tm):
    i = pl.program_id(0)

    @pl.when(i == 0)
    def _():
        carry_ref[...] = jnp.zeros_like(carry_ref)

    zs = _dot(hn_ref[...], ws_ref[...])
    lane = lax.broadcasted_iota(jnp.int32, zs.shape, 1)
    in_a = lane < heads
    in_b = (lane >= heads) & (lane < 2 * heads)
    in_f = (lane >= 2 * heads) & (lane < 3 * heads)
    a_log = prm_ref[0:1, :]
    dt_b = prm_ref[1:2, :]
    f_b = prm_ref[2:3, :]
    log_alpha = jnp.where(in_a, -jnp.exp(a_log) * _softplus(zs + dt_b), 0.0)
    beta = jnp.where(in_b, _sigmoid(zs), 0.0)
    log_f = jnp.where(in_f, -_softplus(-(zs + f_b)), 0.0)

    row = lax.broadcasted_iota(jnp.int32, (tm, tm), 0)
    col = lax.broadcasted_iota(jnp.int32, (tm, tm), 1)
    tri = col <= row
    tri_all = jnp.where(tri, 1.0, 0.0).astype(BF16)
    tri_chunk = jnp.where(tri & ((row // CHUNK) == (col // CHUNK)), 1.0, 0.0).astype(BF16)

    g_cum = _dot_exact_rhs(tri_chunk, log_alpha)
    c_cum = _dot_exact_rhs(tri_all, log_f) + jnp.where(in_f, carry_ref[...], 0.0)
    carry_ref[...] = c_cum[tm - 1:tm, :]

    packed = g_cum + beta + c_cum
    gw = gexp_ref.shape[1]
    gexp_ref[...] = _dot_exact_lhs(packed, sel_ref[:, 0:gw])
    bexp_ref[...] = _dot_exact_lhs(packed, sel_ref[:, gw:2 * gw])
    cexp_ref[...] = _dot_exact_lhs(packed, sel_ref[:, 2 * gw:3 * gw])
    rows_ref[...] = packed.T


def _gates(hn, w_small, l, prm, sel, heads):
    s, d = hn.shape
    gw = heads * HEAD_DIM
    tm = _tile(s, (256, 128, 64))
    exp_spec = pl.BlockSpec((tm, gw), lambda i: (i, 0))
    return pl.pallas_call(
        functools.partial(_gates_kernel, heads=heads, tm=tm),
        out_shape=(jax.ShapeDtypeStruct((s, gw), F32),) * 3 + (jax.ShapeDtypeStruct((LANES, s), F32),),
        grid=(s // tm,),
        in_specs=[pl.BlockSpec((tm, d), lambda i: (i, 0)),
                  pl.BlockSpec((None, d, LANES), lambda i: (l, 0, 0)),
                  pl.BlockSpec((8, LANES), lambda i: (0, 0)),
                  pl.BlockSpec((LANES, 3 * gw), lambda i: (0, 0))],
        out_specs=(exp_spec, exp_spec, exp_spec, pl.BlockSpec((LANES, tm), lambda i: (0, i))),
        scratch_shapes=[pltpu.VMEM((1, LANES), F32)],
        compiler_params=_cparams("arbitrary"),
        name="gates",
    )(hn, w_small, prm, sel)


def _gdn_prep_kernel(q_ref, k_ref, v_ref, qh_ref, kh_ref, vh_ref, cq_ref, ck_ref, cv_ref,
                     gexp_ref, bexp_ref, rows_ref,
                     kw_ref, b_ref, qeff_ref, o0_ref, gl_ref, *, tm):
    i = pl.program_id(0)
    h = pl.program_id(1)
    first = i == 0

    def conv_silu(x_ref, halo_ref, w_ref):
        x = x_ref[...].astype(F32)
        halo = jnp.where(first, 0.0, halo_ref[...].astype(F32))
        xe = jnp.concatenate([halo, x], axis=0)
        w = w_ref[...]
        y = w[SHORT_CONV - 1:SHORT_CONV, :] * x
        for kk in range(SHORT_CONV - 1):
            off = HALO - (SHORT_CONV - 1) + kk
            y = y + w[kk:kk + 1, :] * xe[off:off + tm, :]
        return _silu(y)

    q = conv_silu(q_ref, qh_ref, cq_ref)
    k = conv_silu(k_ref, kh_ref, ck_ref)
    v = conv_silu(v_ref, vh_ref, cv_ref)
    q = q * (lax.rsqrt(jnp.sum(q * q, axis=-1, keepdims=True) + NORM_EPS) * (HEAD_DIM ** -0.5))
    k = k * lax.rsqrt(jnp.sum(k * k, axis=-1, keepdims=True) + NORM_EPS)

    g_all = gexp_ref[...]
    bt_all = bexp_ref[...]
    g_rows = rows_ref[pl.ds(h % 8, 1), :]

    ii = lax.broadcasted_iota(jnp.int32, (CHUNK, CHUNK), 0)
    jj = lax.broadcasted_iota(jnp.int32, (CHUNK, CHUNK), 1)
    causal = jj <= ii
    strict = jj < ii

    sls = [slice(c * CHUNK, (c + 1) * CHUNK) for c in range(tm // CHUNK)]
    gs = [g_all[sl] for sl in sls]
    bts = [bt_all[sl] for sl in sls]
    egs = [jnp.exp(g) for g in gs]
    g_lasts = [g[CHUNK - 1:CHUNK, :] for g in gs]
    decays = [jnp.exp(jnp.where(causal, g[:, 0:CHUNK] - g_rows[:, sl], -jnp.inf)) for g, sl in zip(gs, sls)]
    qkks = [_dot_nt(jnp.concatenate([q[sl], k[sl]], axis=0).astype(BF16), k[sl].astype(BF16)) for sl in sls]
    qks = [(qkk[0:CHUNK] * d).astype(BF16) for qkk, d in zip(qkks, decays)]
    ps = [jnp.where(strict, -(bt[:, 0:CHUNK] * qkk[CHUNK:2 * CHUNK] * d), 0.0).astype(BF16)
          for bt, qkk, d in zip(bts, qkks, decays)]
    xs = [jnp.concatenate([(bt * eg) * k[sl], bt * v[sl]], axis=1) for bt, eg, sl in zip(bts, egs, sls)]
    levels = CHUNK.bit_length() - 1
    for lev in range(levels):
        xs = [x + _dot(p, x.astype(BF16)) for p, x in zip(ps, xs)]
        if lev + 1 < levels:
            ps = [_dot(p, p).astype(BF16) for p in ps]
    wubs = [x.astype(BF16) for x in xs]
    k_decs = [(k[sl] * jnp.exp(gl - g)).astype(BF16) for sl, gl, g in zip(sls, g_lasts, gs)]
    kwbs = [_dot_tn(kd, wub) for kd, wub in zip(k_decs, wubs)]
    qos = [_dot(qk, wub) for qk, wub in zip(qks, wubs)]
    for c, sl in enumerate(sls):
        kw_ref[c] = kwbs[c][:, 0:HEAD_DIM].astype(BF16)
        b_ref[c] = kwbs[c][:, HEAD_DIM:2 * HEAD_DIM].astype(BF16)
        qeff_ref[sl, :] = (q[sl] * egs[c] - qos[c][:, 0:HEAD_DIM]).astype(BF16)
        o0_ref[sl, :] = qos[c][:, HEAD_DIM:2 * HEAD_DIM].astype(BF16)
        gl_ref[c] = jnp.exp(g_lasts[c])


def _gdn_prep(z, conv_w, gexp, bexp, rows, heads):
    s = z.shape[0]
    gw = heads * HEAD_DIM
    nc = s // CHUNK
    tm = _tile(s, (512, 256, 128, 64))
    cpt = tm // CHUNK
    hb = tm // HALO

    def col(off):
        return lambda i, h: (i, off + h)

    def halo(off):
        return lambda i, h: (jnp.maximum(i * hb - 1, 0), off + h)

    def wcol(off):
        return lambda i, h: (0, off + h)

    blk = pl.BlockSpec((tm, HEAD_DIM), col(0))
    return pl.pallas_call(
        functools.partial(_gdn_prep_kernel, tm=tm),
        out_shape=(jax.ShapeDtypeStruct((nc, HEAD_DIM, gw), BF16),
                   jax.ShapeDtypeStruct((nc, HEAD_DIM, gw), BF16),
                   jax.ShapeDtypeStruct((s, gw), BF16),
                   jax.ShapeDtypeStruct((s, gw), BF16),
                   jax.ShapeDtypeStruct((nc, 1, gw), F32)),
        grid=(s // tm, heads),
        in_specs=[pl.BlockSpec((tm, HEAD_DIM), col(0)),
                  pl.BlockSpec((tm, HEAD_DIM), col(heads)),
                  pl.BlockSpec((tm, HEAD_DIM), col(2 * heads)),
                  pl.BlockSpec((HALO, HEAD_DIM), halo(0)),
                  pl.BlockSpec((HALO, HEAD_DIM), halo(heads)),
                  pl.BlockSpec((HALO, HEAD_DIM), halo(2 * heads)),
                  pl.BlockSpec((SHORT_CONV, HEAD_DIM), wcol(0)),
                  pl.BlockSpec((SHORT_CONV, HEAD_DIM), wcol(heads)),
                  pl.BlockSpec((SHORT_CONV, HEAD_DIM), wcol(2 * heads)),
                  blk, blk,
                  pl.BlockSpec((8, tm), lambda i, h: (h // 8, i))],
        out_specs=(pl.BlockSpec((cpt, HEAD_DIM, HEAD_DIM), lambda i, h: (i, 0, h)),
                   pl.BlockSpec((cpt, HEAD_DIM, HEAD_DIM), lambda i, h: (i, 0, h)),
                   blk, blk,
                   pl.BlockSpec((cpt, 1, HEAD_DIM), lambda i, h: (i, 0, h))),
        compiler_params=_cparams("parallel", "parallel"),
        name="gdn_prep",
    )(z, z, z, z, z, z, conv_w, conv_w, conv_w, gexp, bexp, rows)


def _gdn_scan_kernel(kw_ref, b_ref, qeff_ref, o0_ref, gl_ref, gate_ref, gn_ref, o_ref, s_ref,
                     *, cpt, group):
    @pl.when(pl.program_id(1) == 0)
    def _():
        s_ref[...] = jnp.zeros_like(s_ref)

    gn = gn_ref[...]
    for c in range(cpt):
        rows = slice(c * CHUNK, (c + 1) * CHUNK)
        for g in range(group):
            cols = slice(g * HEAD_DIM, (g + 1) * HEAD_DIM)
            st = s_ref[g]
            sb = st.astype(BF16)
            o = _dot(qeff_ref[rows, cols], sb) + o0_ref[rows, cols].astype(F32)
            s_ref[g] = gl_ref[c, :, cols] * st + (b_ref[c, :, cols].astype(F32) - _dot(kw_ref[c, :, cols], sb))
            r = lax.rsqrt(jnp.mean(o * o, axis=-1, keepdims=True) + NORM_EPS)
            o_ref[rows, cols] = ((o * r) * gn * _silu(gate_ref[rows, cols].astype(F32))).astype(o_ref.dtype)


def _gdn_scan(kw, b, qeff, o0, gl, z, gn, heads):
    s = qeff.shape[0]
    gw = heads * HEAD_DIM
    tm = _tile(s, (256, 128, 64))
    cpt = tm // CHUNK
    group = 8 if heads % 8 == 0 else heads
    gwid = group * HEAD_DIM
    gate_off = 3 * gw // gwid
    mat = pl.BlockSpec((cpt, HEAD_DIM, gwid), lambda hg, i: (i, 0, hg))
    blk = pl.BlockSpec((tm, gwid), lambda hg, i: (i, hg))
    return pl.pallas_call(
        functools.partial(_gdn_scan_kernel, cpt=cpt, group=group),
        out_shape=jax.ShapeDtypeStruct((s, gw), BF16),
        grid=(heads // group, s // tm),
        in_specs=[mat, mat, blk, blk,
                  pl.BlockSpec((cpt, 1, gwid), lambda hg, i: (i, 0, hg)),
                  pl.BlockSpec((tm, gwid), lambda hg, i: (i, gate_off + hg)),
                  pl.BlockSpec((1, HEAD_DIM), lambda hg, i: (0, 0))],
        out_specs=blk,
        scratch_shapes=[pltpu.VMEM((group, HEAD_DIM, HEAD_DIM), F32)],
        compiler_params=_cparams("parallel", "arbitrary"),
        name="gdn_scan",
    )(kw, b, qeff, o0, gl, z, gn.reshape(1, HEAD_DIM))


def _fox_prep_kernel(q_ref, k_ref, v_ref, c_ref, qa_ref, ka_ref, vt_ref):
    lane = lax.broadcasted_iota(jnp.int32, c_ref.shape, 1)
    qs = (q_ref[...].astype(F32) * (HEAD_DIM ** -0.5 * LOG2E)).astype(BF16)
    qa_ref[...] = jnp.concatenate([qs, jnp.where(lane < 3, 1.0, 0.0).astype(BF16)], axis=1)
    c1, c2, c3 = _split3(c_ref[...] * (-LOG2E))
    zero = jnp.zeros_like(c1)
    pieces = jnp.where(lane == 0, c1, jnp.where(lane == 1, c2, jnp.where(lane == 2, c3, zero)))
    ka_ref[...] = jnp.concatenate([k_ref[...], pieces], axis=1)
    vt_ref[0:HEAD_DIM, :] = v_ref[...].astype(F32).T.astype(BF16)
    vt_ref[HEAD_DIM:VT_ROWS, :] = jnp.ones((VT_ROWS - HEAD_DIM, vt_ref.shape[1]), BF16)


def _fox_prep(z, cexp, heads):
    s = z.shape[0]
    gw = heads * HEAD_DIM
    ts = _tile(s, (2048, 1024, 512, 256, 128))
    q_off, k_off, v_off = 4 * heads, 5 * heads, 6 * heads
    aug = pl.BlockSpec((ts, 2 * HEAD_DIM), lambda i, h: (i, h))
    return pl.pallas_call(
        _fox_prep_kernel,
        out_shape=(jax.ShapeDtypeStruct((s, 2 * gw), BF16),
                   jax.ShapeDtypeStruct((s, 2 * gw), BF16),
                   jax.ShapeDtypeStruct((heads * VT_ROWS, s), BF16)),
        grid=(s // ts, heads),
        in_specs=[pl.BlockSpec((ts, HEAD_DIM), lambda i, h: (i, q_off + h)),
                  pl.BlockSpec((ts, HEAD_DIM), lambda i, h: (i, k_off + h)),
                  pl.BlockSpec((ts, HEAD_DIM), lambda i, h: (i, v_off + h)),
                  pl.BlockSpec((ts, HEAD_DIM), lambda i, h: (i, h))],
        out_specs=(aug, aug, pl.BlockSpec((VT_ROWS, ts), lambda i, h: (h, i))),
        compiler_params=_cparams("parallel", "parallel"),
        name="fox_prep",
    )(z, z, z, cexp)


def _fox_kernel(qa_ref, ka_ref, vt_ref, o_ref, sa_ref, sb_ref, m_ref, acc_ref, *, tq):
    i = pl.program_id(1)
    tk = tq // 2
    qa = qa_ref[...]
    qw = min(tk, 2 * LANES)

    def scores(j):
        start = pl.multiple_of(j * tk, tk)
        return _dot_nt(ka_ref[pl.ds(start, tk), :], qa)

    def block(s_ref, j, diag_offset):
        vt = vt_ref[:, pl.ds(pl.multiple_of(j * tk, tk), tk)]
        for c0 in range(diag_offset or 0, tq, qw):
            cols = slice(c0, c0 + qw)
            s = s_ref[:, cols]
            if diag_offset is not None:
                key = lax.broadcasted_iota(jnp.int32, (tk, qw), 0) + diag_offset
                qry = lax.broadcasted_iota(jnp.int32, (tk, qw), 1) + c0
                s = jnp.where(key <= qry, s, -jnp.inf)
            m_old = m_ref[:, cols]
            m_new = jnp.maximum(m_old, jnp.max(s, axis=0, keepdims=True))
            p = jnp.exp2(s - m_new).astype(BF16)
            acc_ref[:, cols] = jnp.exp2(m_old - m_new) * acc_ref[:, cols] + _dot(vt, p)
            m_ref[:, cols] = m_new

    m_ref[...] = jnp.full(m_ref.shape, -1e30, F32)
    acc_ref[...] = jnp.zeros_like(acc_ref)
    sa_ref[...] = scores(0)

    def pair(j):
        sb_ref[...] = scores(j + 1)
        block(sa_ref, j, None)
        sa_ref[...] = scores(j + 2)
        block(sb_ref, j + 1, None)

    def body(jj, carry):
        for u in range(4):
            pair(8 * jj + 2 * u)
        return carry

    lax.fori_loop(0, i // 4, body, 0)
    done = i - i % 4

    @pl.when(i % 4 >= 2)
    def _():
        pair(2 * done)
        pair(2 * done + 2)

    @pl.when(i % 2 == 1)
    def _():
        pair(2 * i - 2)

    last = pl.multiple_of((2 * i + 1) * tk, tk)
    sb_ref[:, tk:tq] = _dot_nt(ka_ref[pl.ds(last, tk), :], qa[tk:tq, :])
    block(sa_ref, 2 * i, 0)
    block(sb_ref, 2 * i + 1, tk)

    o = acc_ref[0:HEAD_DIM, :] / acc_ref[HEAD_DIM:HEAD_DIM + 1, :]
    o_ref[...] = o.T.astype(o_ref.dtype)


def _fox(qa, ka, vt, heads):
    s = qa.shape[0]
    gw = heads * HEAD_DIM
    tq = _tile(s, (1024, 512, 256))
    return pl.pallas_call(
        functools.partial(_fox_kernel, tq=tq),
        out_shape=jax.ShapeDtypeStruct((s, gw), BF16),
        grid=(heads, s // tq),
        in_specs=[pl.BlockSpec((tq, 2 * HEAD_DIM), lambda h, i: (i, h)),
                  pl.BlockSpec((s, 2 * HEAD_DIM), lambda h, i: (0, h)),
                  pl.BlockSpec((VT_ROWS, s), lambda h, i: (h, 0))],
        out_specs=pl.BlockSpec((tq, HEAD_DIM), lambda h, i: (i, h)),
        scratch_shapes=[pltpu.VMEM((tq // 2, tq), F32),
                        pltpu.VMEM((tq // 2, tq), F32),
                        pltpu.VMEM((1, tq), F32),
                        pltpu.VMEM((VT_ROWS, tq), F32)],
        compiler_params=_cparams("parallel", "arbitrary"),
        name="fox_attention",
    )(qa, ka, vt)


def _merge_kernel(oa_ref, ob_ref, wa_ref, wb_ref, ga_ref, gb_ref, o_ref):
    ya = _dot(oa_ref[...], wa_ref[...])
    yb = _dot(ob_ref[...], wb_ref[...])
    y = _sigmoid(ga_ref[...].astype(F32)) * ya + _sigmoid(gb_ref[...].astype(F32)) * yb
    o_ref[...] = y.astype(o_ref.dtype)


def _merge(oa, ob, wa, wb, l, z, heads):
    s, gw = oa.shape
    d = wa.shape[2]
    tm = _tile(s, (1024, 512, 256, 128))
    tn = _tile(math.gcd(7 * gw, d), (512, 256, 128))
    ga_off = 7 * gw // tn
    gb_off = (7 * gw + d) // tn
    return pl.pallas_call(
        _merge_kernel,
        out_shape=jax.ShapeDtypeStruct((s, d), BF16),
        grid=(s // tm, d // tn),
        in_specs=[pl.BlockSpec((tm, gw), lambda i, j: (i, 0)),
                  pl.BlockSpec((tm, gw), lambda i, j: (i, 0)),
                  pl.BlockSpec((None, gw, tn), lambda i, j: (l, 0, j)),
                  pl.BlockSpec((None, gw, tn), lambda i, j: (l, 0, j)),
                  pl.BlockSpec((tm, tn), lambda i, j: (i, ga_off + j)),
                  pl.BlockSpec((tm, tn), lambda i, j: (i, gb_off + j))],
        out_specs=pl.BlockSpec((tm, tn), lambda i, j: (i, j)),
        compiler_params=_cparams("parallel", "arbitrary"),
        name="merge",
    )(oa, ob, wa, wb, z, z)


def _ffn_up_kernel(x_ref, wa_ref, wb_ref, ca_ref, cb_ref, o_ref, ta_ref, tb_ref, *, tm, cw):
    @pl.when(pl.program_id(1) == 0)
    def _():
        ta_ref[...] = jnp.zeros_like(ta_ref)
        tb_ref[...] = jnp.zeros_like(tb_ref)

    x = x_ref[...]

    def conv(u, tail_ref, c_ref, cols):
        ue = jnp.concatenate([tail_ref[:, cols], u], axis=0)
        c = c_ref[:, cols]
        y = c[FFN_CONV - 1:FFN_CONV, :] * u
        for kk in range(FFN_CONV - 1):
            off = TAIL - (FFN_CONV - 1) + kk
            y = y + c[kk:kk + 1, :] * ue[off:off + tm, :]
        tail_ref[:, cols] = u[tm - TAIL:tm, :]
        return y

    for c0 in range(0, o_ref.shape[1], cw):
        cols = slice(c0, c0 + cw)
        a = conv(_dot(x, wa_ref[:, cols]), ta_ref, ca_ref, cols)
        b = conv(_dot(x, wb_ref[:, cols]), tb_ref, cb_ref, cols)
        o_ref[:, cols] = (_silu(a) * b).astype(o_ref.dtype)


def _ffn_up(hn, w_up, l, conv_w):
    s, d = hn.shape
    dff = w_up.shape[2] // 2
    tm = _tile(s, (1024, 512, 256, 128, 64))
    tn = _tile(dff, (512, 256, 128))
    cw = tn
    nb = dff // tn
    return pl.pallas_call(
        functools.partial(_ffn_up_kernel, tm=tm, cw=cw),
        out_shape=jax.ShapeDtypeStruct((s, dff), BF16),
        grid=(nb, s // tm),
        in_specs=[pl.BlockSpec((tm, d), lambda j, i: (i, 0)),
                  pl.BlockSpec((None, d, tn), lambda j, i: (l, 0, j)),
                  pl.BlockSpec((None, d, tn), lambda j, i: (l, 0, nb + j)),
                  pl.BlockSpec((FFN_CONV, tn), lambda j, i: (0, j)),
                  pl.BlockSpec((FFN_CONV, tn), lambda j, i: (0, nb + j))],
        out_specs=pl.BlockSpec((tm, tn), lambda j, i: (i, j)),
        scratch_shapes=[pltpu.VMEM((TAIL, tn), F32), pltpu.VMEM((TAIL, tn), F32)],
        compiler_params=_cparams("parallel", "arbitrary"),
        name="ffn_up",
    )(hn, w_up, w_up, conv_w, conv_w)


def _mixer_weights(w_in, heads):
    depth, d, _ = w_in.shape
    gw = heads * HEAD_DIM
    o_small = 4 * gw
    o_fqkv = o_small + 2 * heads
    o_ff = o_fqkv + 3 * gw
    o_gates = o_ff + heads
    w_main = jnp.concatenate(
        [w_in[:, :, 0:o_small], w_in[:, :, o_fqkv:o_ff], w_in[:, :, o_gates:o_gates + 2 * d]],
        axis=2).astype(BF16)
    w_small = jnp.concatenate(
        [w_in[:, :, o_small:o_fqkv], w_in[:, :, o_ff:o_gates],
         jnp.zeros((depth, d, LANES - 3 * heads), w_in.dtype)], axis=2).astype(BF16)
    return w_main, w_small


def _gate_params(a_log, dt_bias, f_bias, heads):
    prm = jnp.zeros((8, LANES), F32)
    prm = prm.at[0, 0:heads].set(a_log.astype(F32))
    prm = prm.at[1, 0:heads].set(dt_bias.astype(F32))
    prm = prm.at[2, 2 * heads:3 * heads].set(f_bias.astype(F32))
    return prm


def _selector(heads):
    gw = heads * HEAD_DIM
    src = lax.broadcasted_iota(jnp.int32, (LANES, 3 * gw), 0)
    dst = lax.broadcasted_iota(jnp.int32, (LANES, 3 * gw), 1)
    want = (dst // gw) * heads + (dst % gw) // HEAD_DIM
    return jnp.where(src == want, 1.0, 0.0).astype(BF16)


def kernel(x, mix_norm, w_in, gdn_conv, gdn_a_log, gdn_dt_bias, gdn_norm, fox_f_bias,
           w_proj_a, w_proj_b, w_out, ffn_norm, w_up, ffn_conv, w_down, final_norm):
    batch, seq, d = x.shape
    depth = w_in.shape[0]
    heads = d // (2 * HEAD_DIM)
    assert 3 * heads <= LANES and seq % CHUNK == 0
    sel = _selector(heads)
    w_main, w_small = _mixer_weights(w_in, heads)
    wa, wb, wo, wu, wd = (w.astype(BF16) for w in (w_proj_a, w_proj_b, w_out, w_up, w_down))
    outs = []
    for bi in range(batch):
        h = x[bi].astype(F32)
        for l in range(depth):
            hn = _rmsnorm(h, mix_norm[l], BF16)
            z = _matmul(hn, w_main, l, BF16, "in_proj")
            prm = _gate_params(gdn_a_log[l], gdn_dt_bias[l], fox_f_bias[l], heads)
            gexp, bexp, cexp, rows = _gates(hn, w_small, l, prm, sel, heads)
            kw, b, qeff, o0, gl = _gdn_prep(z, gdn_conv[l].astype(F32), gexp, bexp, rows, heads)
            oa = _gdn_scan(kw, b, qeff, o0, gl, z, gdn_norm[l].astype(F32), heads)
            qa, ka, vt = _fox_prep(z, cexp, heads)
            ob = _fox(qa, ka, vt, heads)
            y = _merge(oa, ob, wa, wb, l, z, heads)
            h = _matmul_residual(y, wo, l, h, "out_proj")
            hn = _rmsnorm(h, ffn_norm[l], BF16)
            act = _ffn_up(hn, wu, l, ffn_conv[l].astype(F32))
            h = _matmul_residual(act, wd, l, h, "ffn_down")
        outs.append(_rmsnorm(h, final_norm, x.dtype))
    return jnp.stack(outs, axis=0)
```
